```python
import math
import jax, jax.numpy as jnp
from jax import lax
import numpy as np

D_MODEL = 4096
BATCH = 1
SEQ = 16384
DEPTH = 1
DEC_BATCH = 2
DEC_SEQ = 8192
PAST_LEN = 128

N_HEADS = 16
DQK = 64
DV = 2 * DQK
ATT_QK = N_HEADS * 2 * DQK
ATT_WIDTH = N_HEADS * DV
SSM_WIDTH = D_MODEL - ATT_WIDTH
SSM_CH = 16
SSM_GROUPS = SSM_WIDTH // SSM_CH
SSM_STATE = 64
SSM_GROUP_CHUNK = 16
SSM_N_CHUNKS = SSM_GROUPS // SSM_GROUP_CHUNK
PROJ_WIDTH = 2 * ATT_QK + ATT_WIDTH + SSM_WIDTH
Q_BLOCK = 128
N_EXPERT_GROUPS = 4
EXPERTS_PER_GROUP = 8
N_EXPERTS = N_EXPERT_GROUPS * EXPERTS_PER_GROUP
TOP_K_INNER = 2
D_FF_EXPERT = 1024
EXPERT_BLOCK = 128
RMS_EPS = 1e-6

kernel_name = 'hymba_diffattn_s5_hmoe_encoder'


def rms_norm(x, g):
    xf = x.astype(jnp.float32)
    y = xf * lax.rsqrt(jnp.mean(xf * xf, axis=-1, keepdims=True) + RMS_EPS) * g.astype(jnp.float32)
    return y.astype(x.dtype)


def diff_attention(q, k, v, lam, sub_g, lam_init):
    bsz, L = q.shape[0], q.shape[1]
    nb = L // Q_BLOCK
    slopes = jnp.exp2(-8.0 * jnp.arange(1, N_HEADS + 1, dtype=jnp.float32) / N_HEADS)
    kpos = jnp.arange(L, dtype=jnp.float32)
    qb = q.reshape(bsz, nb, Q_BLOCK, N_HEADS, 2, DQK).swapaxes(0, 1)
    starts = jnp.arange(nb, dtype=jnp.int32) * Q_BLOCK
    scale = DQK ** -0.5

    def block(args):
        qblk, start = args
        s = jnp.einsum('bqhcd,bkhcd->bhcqk', qblk, k) * scale
        qpos = start.astype(jnp.float32) + jnp.arange(Q_BLOCK, dtype=jnp.float32)
        bias = -slopes[:, None, None] * jnp.abs(qpos[:, None] - kpos[None, :])
        p = jax.nn.softmax(s + bias[None, :, None], axis=-1)
        pd = p[:, :, 0] - lam * p[:, :, 1]
        return jnp.einsum('bhqk,bkhe->bqhe', pd, v)

    o = lax.map(block, (qb, starts))
    o = o.swapaxes(0, 1).reshape(bsz, L, N_HEADS, DV)
    o = rms_norm(o, sub_g) * (1.0 - lam_init)
    return o.reshape(bsz, L, ATT_WIDTH)


def _cplx_combine(e1, e2):
    a1r, a1i, b1r, b1i = e1
    a2r, a2i, b2r, b2i = e2
    return (a2r * a1r - a2i * a1i,
            a2r * a1i + a2i * a1r,
            a2r * b1r - a2i * b1i + b2r,
            a2r * b1i + a2i * b1r + b2i)


def s5_bidirectional(u, A_re, A_im, log_dt, B_re, B_im, C_re, C_im, d_skip):
    bsz, L = u.shape[0], u.shape[1]
    uf = u.astype(jnp.float32).reshape(bsz, L, SSM_N_CHUNKS, SSM_GROUP_CHUNK, SSM_CH).transpose(2, 0, 1, 3, 4)

    def chunked(p):
        p = p.astype(jnp.float32)
        return jnp.moveaxis(p.reshape(p.shape[0], SSM_N_CHUNKS, SSM_GROUP_CHUNK, *p.shape[2:]), 1, 0)

    dc_all = d_skip.astype(jnp.float32).reshape(SSM_N_CHUNKS, SSM_GROUP_CHUNK, SSM_CH)

    def per_chunk(args):
        uc, a_re, a_im, ldt, b_re, b_im, c_re, c_im, dc = args
        y = dc * uc
        for dirn in range(2):
            dt = jnp.exp(ldt[dirn])[:, None]
            ar, ai = a_re[dirn], a_im[dirn]
            mag = jnp.exp(ar * dt)
            ang = ai * dt
            lr, li = mag * jnp.cos(ang), mag * jnp.sin(ang)
            den = ar * ar + ai * ai
            cr = ((lr - 1.0) * ar + li * ai) / den
            ci = (li * ar - (lr - 1.0) * ai) / den
            bbr = cr[..., None] * b_re[dirn] - ci[..., None] * b_im[dirn]
            bbi = cr[..., None] * b_im[dirn] + ci[..., None] * b_re[dirn]
            xr = jnp.einsum('blgh,gph->blgp', uc, bbr)
            xi = jnp.einsum('blgh,gph->blgp', uc, bbi)
            elems = (jnp.broadcast_to(lr, xr.shape), jnp.broadcast_to(li, xr.shape), xr, xi)
            _, _, sr, si = lax.associative_scan(_cplx_combine, elems, axis=1, reverse=(dirn == 1))
            y = y + jnp.einsum('blgp,ghp->blgh', sr, c_re[dirn]) - jnp.einsum('blgp,ghp->blgh', si, c_im[dirn])
        return y

    ys = lax.map(per_chunk, (uf, chunked(A_re), chunked(A_im), chunked(log_dt), chunked(B_re), chunked(B_im),
                             chunked(C_re), chunked(C_im), dc_all))
    return ys.transpose(1, 2, 0, 3, 4).reshape(bsz, L, SSM_WIDTH)


def hierarchical_moe(xn, rg_w, rg_b, re_w, re_b, w_gate, w_up, w_down):
    bsz, L, D = xn.shape
    T = bsz * L
    xf = xn.reshape(T, D)
    gp = jax.nn.softmax((xf @ rg_w + rg_b).astype(jnp.float32), axis=-1)
    gsel = jnp.argmax(gp, axis=-1).astype(jnp.int32)
    gprob = jnp.take_along_axis(gp, gsel[:, None], axis=1)[:, 0]
    elog = (xf @ re_w + re_b).astype(jnp.float32).reshape(T, N_EXPERT_GROUPS, EXPERTS_PER_GROUP)
    elog = jnp.take_along_axis(elog, gsel[:, None, None], axis=1)[:, 0]
    topv, topi = lax.top_k(jax.nn.softmax(elog, axis=-1), TOP_K_INNER)
    gate = topv / jnp.sum(topv, axis=-1, keepdims=True) * gprob[:, None]
    eid = (gsel[:, None] * EXPERTS_PER_GROUP + topi.astype(jnp.int32)).reshape(-1)
    flat_w = gate.reshape(-1)
    flat_tok = jnp.repeat(jnp.arange(T, dtype=jnp.int32), TOP_K_INNER)
    N = T * TOP_K_INNER
    order = jnp.argsort(eid)
    se, sw, stok = eid[order], flat_w[order], flat_tok[order]
    counts = jnp.bincount(eid, length=N_EXPERTS).astype(jnp.int32)
    starts = jnp.cumsum(counts) - counts
    pcounts = ((counts + EXPERT_BLOCK - 1) // EXPERT_BLOCK) * EXPERT_BLOCK
    pends = jnp.cumsum(pcounts)
    pstarts = pends - pcounts
    dest = pstarts[se] + (jnp.arange(N, dtype=jnp.int32) - starts[se])
    n_blocks = -(-N // EXPERT_BLOCK) + N_EXPERTS
    slot_tok = jnp.zeros((n_blocks * EXPERT_BLOCK,), jnp.int32).at[dest].set(stok)
    slot_w = jnp.zeros((n_blocks * EXPERT_BLOCK,), jnp.float32).at[dest].set(sw)
    blk_e = jnp.clip(jnp.searchsorted(pends, jnp.arange(n_blocks, dtype=jnp.int32) * EXPERT_BLOCK, side='right'),
                     0, N_EXPERTS - 1).astype(jnp.int32)

    def step(acc, inp):
        tok, wt, e = inp
        xb = xf[tok]
        h = jax.nn.silu(xb @ w_gate[e]) * (xb @ w_up[e])
        yb = (h @ w_down[e]) * wt[:, None]
        return acc.at[tok].add(yb.astype(acc.dtype)), None

    acc, _ = lax.scan(step, jnp.zeros((T, D), xf.dtype),
                      (slot_tok.reshape(n_blocks, EXPERT_BLOCK), slot_w.reshape(n_blocks, EXPERT_BLOCK), blk_e))
    return acc.reshape(bsz, L, D)


def encoder_layer(x, lam_init, norm1_g, w_in, q_norm_g, k_norm_g, lam_q1, lam_k1, lam_q2, lam_k2, attn_sub_g,
                  A_re, A_im, log_dt, B_re, B_im, C_re, C_im, d_skip, glu_w, glu_b, ssm_out_g, w_out,
                  norm2_g, rg_w, rg_b, re_w, re_b, w_gate, w_up, w_down):
    bsz, L = x.shape[0], x.shape[1]
    xn = rms_norm(x, norm1_g)
    proj = xn @ w_in
    q, k, v, u = jnp.split(proj, [ATT_QK, 2 * ATT_QK, 2 * ATT_QK + ATT_WIDTH], axis=-1)
    q = rms_norm(q.reshape(bsz, L, N_HEADS, 2, DQK).astype(jnp.float32), q_norm_g)
    k = rms_norm(k.reshape(bsz, L, N_HEADS, 2, DQK).astype(jnp.float32), k_norm_g)
    v = v.reshape(bsz, L, N_HEADS, DV).astype(jnp.float32)
    lam = (jnp.exp(jnp.sum(lam_q1.astype(jnp.float32) * lam_k1.astype(jnp.float32)))
           - jnp.exp(jnp.sum(lam_q2.astype(jnp.float32) * lam_k2.astype(jnp.float32))) + lam_init)
    att = diff_attention(q, k, v, lam, attn_sub_g, lam_init).astype(x.dtype)
    y = s5_bidirectional(u, A_re, A_im, log_dt, B_re, B_im, C_re, C_im, d_skip)
    g = jax.nn.gelu(y)
    ssm = g * jax.nn.sigmoid(g @ glu_w.astype(jnp.float32) + glu_b.astype(jnp.float32))
    ssm = rms_norm(ssm, ssm_out_g).astype(x.dtype)
    h = x + jnp.concatenate([att, ssm], axis=-1) @ w_out
    return h + hierarchical_moe(rms_norm(h, norm2_g), rg_w, rg_b, re_w, re_b, w_gate, w_up, w_down)


def setup_inputs(seed: int = 0) -> dict:
    key = jax.random.key(seed)
    ks = jax.random.split(key, 32)
    f32 = jnp.float32
    nrm = lambda k, shape, s: jax.random.normal(k, shape, f32) * s
    gain = lambda k, shape: 1.0 + 0.02 * jax.random.normal(k, shape, f32)
    G, P, CH = SSM_GROUPS, SSM_STATE, SSM_CH
    a_im0 = math.pi * jnp.arange(P, dtype=f32)
    return {
        'x_prompt': jax.random.normal(ks[0], (BATCH, SEQ, D_MODEL), f32),
        'x_sample': jax.random.normal(ks[1], (DEC_BATCH, DEC_SEQ, D_MODEL), f32),
        'norm1_g': gain(ks[2], (DEPTH, D_MODEL)),
        'w_in': nrm(ks[3], (DEPTH, D_MODEL, PROJ_WIDTH), D_MODEL ** -0.5),
        'q_norm_g': gain(ks[4], (DEPTH, DQK)),
        'k_norm_g': gain(ks[5], (DEPTH, DQK)),
        'lam_q1': nrm(ks[6], (DEPTH, DQK), 0.1),
        'lam_k1': nrm(ks[7], (DEPTH, DQK), 0.1),
        'lam_q2': nrm(ks[8], (DEPTH, DQK), 0.1),
        'lam_k2': nrm(ks[9], (DEPTH, DQK), 0.1),
        'attn_sub_g': gain(ks[10], (DEPTH, DV)),
        'ssm_A_re': -0.5 + nrm(ks[11], (DEPTH, 2, G, P), 0.01),
        'ssm_A_im': a_im0 + nrm(ks[12], (DEPTH, 2, G, P), 0.01),
        'ssm_log_dt': jax.random.uniform(ks[13], (DEPTH, 2, G), f32, math.log(1e-3), math.log(1e-1)),
        'ssm_B_re': nrm(ks[14], (DEPTH, 2, G, P, CH), (2 * CH) ** -0.5),
        'ssm_B_im': nrm(ks[15], (DEPTH, 2, G, P, CH), (2 * CH) ** -0.5),
        'ssm_C_re': nrm(ks[16], (DEPTH, 2, G, CH, P), P ** -0.5),
        'ssm_C_im': nrm(ks[17], (DEPTH, 2, G, CH, P), P ** -0.5),
        'ssm_D': nrm(ks[18], (DEPTH, G, CH), 1.0),
        'glu_w': nrm(ks[19], (DEPTH, SSM_WIDTH, SSM_WIDTH), SSM_WIDTH ** -0.5),
        'glu_b': nrm(ks[20], (DEPTH, SSM_WIDTH), 0.02),
        'ssm_out_g': gain(ks[21], (DEPTH, SSM_WIDTH)),
        'w_out': nrm(ks[22], (DEPTH, D_MODEL, D_MODEL), D_MODEL ** -0.5),
        'norm2_g': gain(ks[23], (DEPTH, D_MODEL)),
        'router_group_w': nrm(ks[24], (DEPTH, D_MODEL, N_EXPERT_GROUPS), D_MODEL ** -0.5),
        'router_group_b': nrm(ks[25], (DEPTH, N_EXPERT_GROUPS), 0.01),
        'router_expert_w': nrm(ks[26], (DEPTH, D_MODEL, N_EXPERTS), D_MODEL ** -0.5),
        'router_expert_b': nrm(ks[27], (DEPTH, N_EXPERTS), 0.01),
        'exp_w_gate': nrm(ks[28], (DEPTH, N_EXPERTS, D_MODEL, D_FF_EXPERT), D_MODEL ** -0.5),
        'exp_w_up': nrm(ks[29], (DEPTH, N_EXPERTS, D_MODEL, D_FF_EXPERT), D_MODEL ** -0.5),
        'exp_w_down': nrm(ks[30], (DEPTH, N_EXPERTS, D_FF_EXPERT, D_MODEL), D_FF_EXPERT ** -0.5),
    }


def reference(x_prompt, x_sample, norm1_g, w_in, q_norm_g, k_norm_g, lam_q1, lam_k1, lam_q2, lam_k2, attn_sub_g,
              ssm_A_re, ssm_A_im, ssm_log_dt, ssm_B_re, ssm_B_im, ssm_C_re, ssm_C_im, ssm_D, glu_w, glu_b,
              ssm_out_g, w_out, norm2_g, router_group_w, router_group_b, router_expert_w, router_expert_b,
              exp_w_gate, exp_w_up, exp_w_down):
    def run(x):
        for l in range(DEPTH):
            lam_init = 0.8 - 0.6 * math.exp(-0.3 * l)
            x = encoder_layer(x, lam_init, norm1_g[l], w_in[l], q_norm_g[l], k_norm_g[l], lam_q1[l], lam_k1[l],
                              lam_q2[l], lam_k2[l], attn_sub_g[l], ssm_A_re[l], ssm_A_im[l], ssm_log_dt[l],
                              ssm_B_re[l], ssm_B_im[l], ssm_C_re[l], ssm_C_im[l], ssm_D[l], glu_w[l], glu_b[l],
                              ssm_out_g[l], w_out[l], norm2_g[l], router_group_w[l], router_group_b[l],
                              router_expert_w[l], router_expert_b[l], exp_w_gate[l], exp_w_up[l], exp_w_down[l])
        return x
    y_prompt = run(x_prompt)
    y_sample = run(x_sample)
    return (y_prompt, y_sample)
```

```python
import functools
import math

import jax
import jax.numpy as jnp
from jax import lax
from jax.experimental import pallas as pl
from jax.experimental.pallas import tpu as pltpu

F32 = jnp.float32
BF16 = jnp.bfloat16

RMS_EPS = 1e-6
LOG2E = 1.4426950408889634
LANES = 128
VMEM_LIMIT = 56 * 1024 * 1024

N_HEADS = 16
DQK = 64
DV = 2 * DQK
SSM_CH = 16
SSM_STATE = 64
S5_CHUNK = 16
N_EXPERT_GROUPS = 4
EXPERTS_PER_GROUP = 8
N_EXPERTS = N_EXPERT_GROUPS * EXPERTS_PER_GROUP
TOP_K_INNER = 2
ROUTER_PAD = LANES


def _cparams(sem):
    return pltpu.CompilerParams(dimension_semantics=sem, vmem_limit_bytes=VMEM_LIMIT)


def _rmsnorm_kernel(x_ref, g_ref, o_ref):
    x = x_ref[...]
    ms = jnp.mean(x * x, axis=-1, keepdims=True)
    o_ref[...] = (x * lax.rsqrt(ms + RMS_EPS) * g_ref[...]).astype(o_ref.dtype)


def _rmsnorm_cast(x, g, tm=512):
    T, D = x.shape
    tm = min(tm, T)
    return pl.pallas_call(
        _rmsnorm_kernel,
        grid=(T // tm,),
        in_specs=[pl.BlockSpec((tm, D), lambda i: (i, 0)),
                  pl.BlockSpec((1, D), lambda i: (0, 0))],
        out_specs=pl.BlockSpec((tm, D), lambda i: (i, 0)),
        out_shape=jax.ShapeDtypeStruct((T, D), BF16),
        compiler_params=_cparams(("parallel",)),
        name="rmsnorm1",
    )(x, g.reshape(1, D).astype(F32))


def _mm_kernel(a_ref, b_ref, o_ref):
    o_ref[...] = jnp.dot(a_ref[...], b_ref[...], preferred_element_type=F32).astype(o_ref.dtype)


def _matmul(a, b, col0, ncols, out_dtype, name, tm=1024, tn=512):
    M, K = a.shape
    tm = min(tm, M)
    tn = min(tn, ncols)
    cb = col0 // tn
    return pl.pallas_call(
        _mm_kernel,
        grid=(M // tm, ncols // tn),
        in_specs=[pl.BlockSpec((tm, K), lambda i, j: (i, 0)),
                  pl.BlockSpec((K, tn), lambda i, j: (0, j + cb))],
        out_specs=pl.BlockSpec((tm, tn), lambda i, j: (i, j)),
        out_shape=jax.ShapeDtypeStruct((M, ncols), out_dtype),
        compiler_params=_cparams(("parallel", "arbitrary")),
        name=name,
    )(a, b)


def _qk_prep_kernel(x_ref, gq_ref, gk_ref, q_ref, k_ref, *, n_tiles):
    lane = lax.broadcasted_iota(jnp.int32, (1, LANES), 1)
    left = lane < DQK
    for c in range(2 * n_tiles):
        x = x_ref[:, c * LANES:(c + 1) * LANES]
        x2 = x * x
        s_all = jnp.sum(x2, axis=-1, keepdims=True)
        s_left = jnp.sum(jnp.where(left, x2, 0.0), axis=-1, keepdims=True)
        ms = jnp.where(left, s_left, s_all - s_left) * (1.0 / DQK)
        if c < n_tiles:
            q_ref[:, c * LANES:(c + 1) * LANES] = (x * lax.rsqrt(ms + RMS_EPS) * gq_ref[...]).astype(BF16)
        else:
            cc = c - n_tiles
            k_ref[:, cc * LANES:(cc + 1) * LANES] = (x * lax.rsqrt(ms + RMS_EPS) * gk_ref[...]).astype(BF16)


def _qk_prep(qk, gq, gk, tm=256):
    T, W2 = qk.shape
    W = W2 // 2
    tm = min(tm, T)
    gq2 = jnp.tile(gq.astype(F32) * (DQK ** -0.5 * LOG2E), 2).reshape(1, LANES)
    gk2 = jnp.tile(gk.astype(F32), 2).reshape(1, LANES)
    return pl.pallas_call(
        functools.partial(_qk_prep_kernel, n_tiles=W // LANES),
        grid=(T // tm,),
        in_specs=[pl.BlockSpec((tm, W2), lambda i: (i, 0)),
                  pl.BlockSpec((1, LANES), lambda i: (0, 0)),
                  pl.BlockSpec((1, LANES), lambda i: (0, 0))],
        out_specs=[pl.BlockSpec((tm, W), lambda i: (i, 0)),
                   pl.BlockSpec((tm, W), lambda i: (i, 0))],
        out_shape=[jax.ShapeDtypeStruct((T, W), BF16), jax.ShapeDtypeStruct((T, W), BF16)],
        compiler_params=_cparams(("parallel",)),
        name="qk_norm",
    )(qk, gq2, gk2)


def _attn_kernel(lam_ref, slope_ref, q_ref, k_ref, v_ref, g_ref, o_ref, *, tq, tk, nk, out_scale):
    h = pl.program_id(1)
    qi = pl.program_id(2)
    slope = slope_ref[h]
    lam = lam_ref[0]
    q = q_ref[...]
    lane = lax.broadcasted_iota(jnp.int32, (1, LANES), 1)
    zero = jnp.zeros_like(q)
    q1 = jnp.where(lane < DQK, q, zero)
    q2 = jnp.where(lane >= DQK, q, zero)
    rowpos = (qi * tq + lax.broadcasted_iota(jnp.int32, (tq, 1), 0)).astype(F32)
    colrel = lax.broadcasted_iota(jnp.int32, (1, tk), 1).astype(F32)
    nt = (((1,), (1,)), ((), ()))

    def one_map(qz, k, v, bias, m, l, a):
        s = lax.dot_general(qz, k, nt, preferred_element_type=F32) + bias
        m_new = jnp.maximum(m, jnp.max(s, axis=-1, keepdims=True))
        alpha = jnp.exp2(m - m_new)
        p = jnp.exp2(s - m_new)
        l = alpha * l + jnp.sum(p, axis=-1, keepdims=True)
        a = alpha * a + jnp.dot(p.astype(BF16), v, preferred_element_type=F32)
        return m_new, l, a

    def body(ki, carry):
        m1, l1, a1, m2, l2, a2 = carry
        start = pl.multiple_of(ki * tk, tk)
        k = k_ref[pl.ds(start, tk), :]
        v = v_ref[pl.ds(start, tk), :]
        colpos = colrel + lax.convert_element_type(ki * tk, F32)
        bias = -slope * jnp.abs(rowpos - colpos)
        m1, l1, a1 = one_map(q1, k, v, bias, m1, l1, a1)
        m2, l2, a2 = one_map(q2, k, v, bias, m2, l2, a2)
        return m1, l1, a1, m2, l2, a2

    m0 = jnp.full((tq, 1), -1e30, F32)
    l0 = jnp.zeros((tq, 1), F32)
    a0 = jnp.zeros((tq, DV), F32)
    m1, l1, a1, m2, l2, a2 = lax.fori_loop(0, nk, body, (m0, l0, a0, m0, l0, a0))
    o = a1 / l1 - lam * (a2 / l2)
    ms = jnp.mean(o * o, axis=-1, keepdims=True)
    o_ref[...] = (o * lax.rsqrt(ms + RMS_EPS) * (g_ref[...] * out_scale)).astype(o_ref.dtype)


def _attention(qn, kn, vu, lam, sub_g, lam_init, nb, L, tq=256, tk=512):
    T = qn.shape[0]
    tq = min(tq, L)
    tk = min(tk, L)
    nq = L // tq
    slopes = jnp.exp2(-8.0 * jnp.arange(1, N_HEADS + 1, dtype=F32) / N_HEADS) * LOG2E
    kern = functools.partial(_attn_kernel, tq=tq, tk=tk, nk=L // tk, out_scale=1.0 - lam_init)
    return pl.pallas_call(
        kern,
        grid=(nb, N_HEADS, nq),
        in_specs=[pl.BlockSpec(memory_space=pltpu.SMEM),
                  pl.BlockSpec(memory_space=pltpu.SMEM),
                  pl.BlockSpec((tq, LANES), lambda b, h, i: (b * nq + i, h)),
                  pl.BlockSpec((L, LANES), lambda b, h, i: (b, h)),
                  pl.BlockSpec((L, LANES), lambda b, h, i: (b, h)),
                  pl.BlockSpec((1, DV), lambda b, h, i: (0, 0))],
        out_specs=pl.BlockSpec((tq, LANES), lambda b, h, i: (b * nq + i, h)),
        out_shape=jax.ShapeDtypeStruct((T, N_HEADS * DV), BF16),
        compiler_params=_cparams(("parallel", "parallel", "arbitrary")),
        name="diff_attn",
    )(lam.reshape(1).astype(F32), slopes, qn, kn, vu, sub_g.reshape(1, DV).astype(F32))


def _s5_prep(A_re, A_im, log_dt, B_re, B_im, C_re, C_im, d_skip, nc):
    tc = S5_CHUNK
    hp = lax.Precision.HIGHEST
    A_re, A_im, log_dt = A_re.astype(F32), A_im.astype(F32), log_dt.astype(F32)
    B_re, B_im, C_re, C_im = B_re.astype(F32), B_im.astype(F32), C_re.astype(F32), C_im.astype(F32)
    G = A_re.shape[1]
    dt = jnp.exp(log_dt)[..., None]
    ang = A_im * dt
    mag = jnp.exp(A_re * dt)
    lr, li = mag * jnp.cos(ang), mag * jnp.sin(ang)
    den = A_re * A_re + A_im * A_im
    cr = ((lr - 1.0) * A_re + li * A_im) / den
    ci = (li * A_re - (lr - 1.0) * A_im) / den
    bbr = cr[..., None] * B_re - ci[..., None] * B_im
    bbi = cr[..., None] * B_im + ci[..., None] * B_re

    def lam_pow(n):
        nn = n.astype(F32)[:, None, None, None]
        pm = jnp.exp(A_re * dt * nn)
        pa = ang * nn
        return pm * jnp.cos(pa), pm * jnp.sin(pa)

    pr, pi = lam_pow(jnp.arange(tc + 1))
    lbr = pr[..., None] * bbr - pi[..., None] * bbi
    lbi = pr[..., None] * bbi + pi[..., None] * bbr
    kk = (jnp.einsum('dghp,ndgpb->ndghb', C_re, lbr, precision=hp)
          - jnp.einsum('dghp,ndgpb->ndghb', C_im, lbi, precision=hp))
    j = jnp.arange(tc)[:, None]
    t = jnp.arange(tc)[None, :]
    kf = jnp.where((t >= j)[:, :, None, None, None], kk[jnp.clip(t - j, 0, tc), 0], 0.0)
    kb = jnp.where((j >= t)[:, :, None, None, None], kk[jnp.clip(j - t, 0, tc), 1], 0.0)
    eye_t = (j == t).astype(F32)[:, :, None, None, None]
    eye_c = jnp.eye(SSM_CH, dtype=F32)[None, None, None]
    dd = eye_t * eye_c * d_skip.astype(F32)[None, None, :, :, None]
    ty = (kf + kb + dd).transpose(2, 0, 4, 1, 3).reshape(G, tc * SSM_CH, tc * SSM_CH)

    def state_cols(x):
        return x.transpose(1, 0, 3, 2).reshape(G, tc * SSM_CH, SSM_STATE)

    nf = tc - 1 - jnp.arange(tc)
    nbk = jnp.arange(tc)
    w1 = jnp.concatenate([ty, state_cols(lbr[nf, 0]), state_cols(lbi[nf, 0]),
                          state_cols(lbr[nbk, 1]), state_cols(lbi[nbk, 1])], axis=-1)

    def carry_rows(n, d):
        a = C_re[d][None] * pr[n, d][:, :, None, :] - C_im[d][None] * pi[n, d][:, :, None, :]
        bco = C_re[d][None] * pi[n, d][:, :, None, :] + C_im[d][None] * pr[n, d][:, :, None, :]
        to_rows = lambda x: x.transpose(1, 3, 0, 2).reshape(G, SSM_STATE, tc * SSM_CH)
        return to_rows(a), to_rows(-bco)

    fa, fb = carry_rows(jnp.arange(tc) + 1, 0)
    ba, bb = carry_rows(tc - jnp.arange(tc), 1)
    w2 = jnp.concatenate([fa, fb, ba, bb], axis=1)

    nsteps = max(1, int(math.ceil(math.log2(nc))))
    sr, si = lam_pow(tc * (2 ** jnp.arange(nsteps)))
    rows = jnp.stack([jnp.concatenate([sr, sr], -1), jnp.concatenate([-si, si], -1)], axis=2)
    coef = rows.transpose(3, 0, 1, 2, 4).reshape(G, nsteps * 4, LANES)
    return w1.astype(BF16), w2.astype(BF16), coef, nsteps


def _s5_kernel(u_ref, w1_ref, w2_ref, c_ref, o_ref, *, nc, nsteps):
    ny = S5_CHUNK * SSM_CH
    r = jnp.dot(u_ref[0], w1_ref[0], preferred_element_type=F32)
    y = r[:, :ny]
    row = lax.broadcasted_iota(jnp.int32, (nc, 1), 0)

    def shifted(x, d, fwd):
        if fwd:
            return jnp.where(row >= d, pltpu.roll(x, d, axis=0), 0.0)
        return jnp.where(row < nc - d, pltpu.roll(x, nc - d, axis=0), 0.0)

    def carry_in(x, dirn):
        s = shifted(x, 1, dirn == 0)
        for st in range(nsteps):
            d = 2 ** st
            if d >= nc:
                break
            sh = shifted(s, d, dirn == 0)
            a = c_ref[0, 4 * st + 2 * dirn:4 * st + 2 * dirn + 1, :]
            b = c_ref[0, 4 * st + 2 * dirn + 1:4 * st + 2 * dirn + 2, :]
            s = s + a * sh + b * pltpu.roll(sh, SSM_STATE, axis=1)
        return s

    sf = carry_in(r[:, ny:ny + LANES], 0)
    sb = carry_in(r[:, ny + LANES:ny + 2 * LANES], 1)
    s = jnp.concatenate([sf, sb], axis=1).astype(BF16)
    o_ref[0] = y + jnp.dot(s, w2_ref[0], preferred_element_type=F32)


def _s5(u, prep, nb, L):
    w1, w2, coef, nsteps = prep
    T, W = u.shape
    G = W // SSM_CH
    tc = S5_CHUNK
    nc = L // tc
    ny = tc * SSM_CH
    ug = u.reshape(nb * nc, tc, G, SSM_CH).transpose(2, 0, 1, 3).reshape(G, nb * nc, ny)
    yg = pl.pallas_call(
        functools.partial(_s5_kernel, nc=nc, nsteps=nsteps),
        grid=(nb, G),
        in_specs=[pl.BlockSpec((1, nc, ny), lambda b, g: (g, b, 0)),
                  pl.BlockSpec((1, ny, 2 * ny), lambda b, g: (g, 0, 0)),
                  pl.BlockSpec((1, ny, ny), lambda b, g: (g, 0, 0)),
                  pl.BlockSpec((1, 4 * nsteps, LANES), lambda b, g: (g, 0, 0))],
        out_specs=pl.BlockSpec((1, nc, ny), lambda b, g: (g, b, 0)),
        out_shape=jax.ShapeDtypeStruct((G, nb * nc, ny), F32),
        compiler_params=_cparams(("parallel", "parallel")),
        name="s5_chunked",
    )(ug, w1, w2, coef)
    return yg.reshape(G, nb * nc, tc, SSM_CH).transpose(1, 2, 0, 3).reshape(T, W)


def _glu_kernel(y_ref, w_ref, b_ref, g_ref, o_ref):
    y = y_ref[...]
    c0 = math.sqrt(2.0 / math.pi)
    gl = 0.5 * y * (1.0 + jnp.tanh(c0 * (y + 0.044715 * (y * y * y))))
    z = jnp.dot(gl.astype(BF16), w_ref[...], preferred_element_type=F32) + b_ref[...]
    s = gl * (1.0 / (1.0 + jnp.exp(-z)))
    ms = jnp.mean(s * s, axis=-1, keepdims=True)
    o_ref[...] = (s * lax.rsqrt(ms + RMS_EPS) * g_ref[...]).astype(o_ref.dtype)


def _glu_norm(y, w, b, g, tm=512):
    T, W = y.shape
    tm = min(tm, T)
    return pl.pallas_call(
        _glu_kernel,
        grid=(T // tm,),
        in_specs=[pl.BlockSpec((tm, W), lambda i: (i, 0)),
                  pl.BlockSpec((W, W), lambda i: (0, 0)),
                  pl.BlockSpec((1, W), lambda i: (0, 0)),
                  pl.BlockSpec((1, W), lambda i: (0, 0))],
        out_specs=pl.BlockSpec((tm, W), lambda i: (i, 0)),
        out_shape=jax.ShapeDtypeStruct((T, W), BF16),
        compiler_params=_cparams(("parallel",)),
        name="glu_norm",
    )(y, w, b.reshape(1, W).astype(F32), g.reshape(1, W).astype(F32))


def _out_proj_kernel(a_ref, s_ref, wa_ref, ws_ref, x_ref, o_ref):
    acc = jnp.dot(a_ref[...], wa_ref[...], preferred_element_type=F32)
    acc = acc + jnp.dot(s_ref[...], ws_ref[...], preferred_element_type=F32)
    o_ref[...] = x_ref[...] + acc


def _out_proj(att, ssm, w, x, tm=1024, tn=512):
    T, Wa = att.shape
    D = x.shape[1]
    tm = min(tm, T)
    tn = min(tn, D)
    rb = Wa // Wa
    return pl.pallas_call(
        _out_proj_kernel,
        grid=(T // tm, D // tn),
        in_specs=[pl.BlockSpec((tm, Wa), lambda i, j: (i, 0)),
                  pl.BlockSpec((tm, Wa), lambda i, j: (i, 0)),
                  pl.BlockSpec((Wa, tn), lambda i, j: (0, j)),
                  pl.BlockSpec((Wa, tn), lambda i, j: (rb, j)),
                  pl.BlockSpec((tm, tn), lambda i, j: (i, j))],
        out_specs=pl.BlockSpec((tm, tn), lambda i, j: (i, j)),
        out_shape=jax.ShapeDtypeStruct((T, D), F32),
        compiler_params=_cparams(("parallel", "arbitrary")),
        name="out_proj",
    )(att, ssm, w, w, x)


def _router_kernel(h_ref, g_ref, w_ref, b_ref, xn_ref, lg_ref):
    x = h_ref[...]
    ms = jnp.mean(x * x, axis=-1, keepdims=True)
    xn = x * lax.rsqrt(ms + RMS_EPS) * g_ref[...]
    xn_ref[...] = xn
    lg_ref[...] = jnp.dot(xn, w_ref[...], preferred_element_type=F32,
                          precision=lax.Precision.HIGHEST) + b_ref[...]


def _norm_router(h, g, rw, rb, tm=256):
    T, D = h.shape
    tm = min(tm, T)
    return pl.pallas_call(
        _router_kernel,
        grid=(T // tm,),
        in_specs=[pl.BlockSpec((tm, D), lambda i: (i, 0)),
                  pl.BlockSpec((1, D), lambda i: (0, 0)),
                  pl.BlockSpec((D, ROUTER_PAD), lambda i: (0, 0)),
                  pl.BlockSpec((1, ROUTER_PAD), lambda i: (0, 0))],
        out_specs=[pl.BlockSpec((tm, D), lambda i: (i, 0)),
                   pl.BlockSpec((tm, ROUTER_PAD), lambda i: (i, 0))],
        out_shape=[jax.ShapeDtypeStruct((T, D), F32), jax.ShapeDtypeStruct((T, ROUTER_PAD), F32)],
        compiler_params=_cparams(("parallel",)),
        name="norm_router",
    )(h, g.reshape(1, D).astype(F32), rw, rb)


def _row_copy(src_hbm, src_row, dst, dst_row, sem):
    return pltpu.make_async_copy(src_hbm.at[pl.ds(src_row, 1), :], dst.at[pl.ds(dst_row, 1), :], sem)


def _gather_start(idx_ref, src_hbm, dst, sem, n):
    def body(r, c):
        _row_copy(src_hbm, idx_ref[0, 0, r], dst, r, sem).start()
        return c
    lax.fori_loop(0, n, body, 0)


def _gather_wait(src_hbm, dst, sem, n):
    def body(r, c):
        _row_copy(src_hbm, 0, dst, r, sem).wait()
        return c
    lax.fori_loop(0, n, body, 0)


def _expert_kernel(be_ref, nu_ref, tok_ref, tokn_ref, sw_ref, x_hbm, wg_ref, wu_ref, wd_ref, o_ref,
                   xbuf, xb, sem, *, blk, nf):
    b = pl.program_id(0)
    f = pl.program_id(1)
    nused = nu_ref[0]
    slot = b % 2
    first = f == 0

    @pl.when(jnp.logical_and(b == 0, first))
    def _():
        _gather_start(tok_ref, x_hbm, xbuf.at[0], sem.at[0], blk)

    @pl.when(jnp.logical_and(first, b < nused))
    def _():
        _gather_wait(x_hbm, xbuf.at[slot], sem.at[slot], blk)
        xb[...] = xbuf[slot].astype(BF16)

    @pl.when(jnp.logical_and(first, b + 1 < nused))
    def _():
        _gather_start(tokn_ref, x_hbm, xbuf.at[1 - slot], sem.at[1 - slot], blk)

    @pl.when(b < nused)
    def _():
        x = xb[...]
        g = jnp.dot(x, wg_ref[0], preferred_element_type=F32)
        u = jnp.dot(x, wu_ref[0], preferred_element_type=F32)
        hh = (g * (1.0 / (1.0 + jnp.exp(-g))) * u).astype(BF16)
        contrib = jnp.dot(hh, wd_ref[0], preferred_element_type=F32)

        @pl.when(first)
        def _():
            o_ref[...] = contrib

        @pl.when(f > 0)
        def _():
            o_ref[...] += contrib

        @pl.when(f == nf - 1)
        def _():
            o_ref[...] = o_ref[...] * sw_ref[0]

    @pl.when(jnp.logical_and(first, b >= nused))
    def _():
        o_ref[...] = jnp.zeros_like(o_ref)


def _experts(xn2, blk_e, nused, slot_tok, slot_w, wg, wu, wd, blk, tf=512):
    T, D = xn2.shape
    nblk = slot_tok.shape[0]
    dff = wg.shape[2]
    tf = min(tf, dff)
    nf = dff // tf

    def wmap_in(b, f, be, nu):
        live = b < nu[0]
        return (be[jnp.minimum(b, nu[0] - 1)], 0, jnp.where(live, f, nf - 1))

    def wmap_out(b, f, be, nu):
        live = b < nu[0]
        return (be[jnp.minimum(b, nu[0] - 1)], jnp.where(live, f, nf - 1), 0)

    grid_spec = pltpu.PrefetchScalarGridSpec(
        num_scalar_prefetch=2,
        grid=(nblk, nf),
        in_specs=[pl.BlockSpec((1, 1, blk), lambda b, f, be, nu: (b, 0, 0), memory_space=pltpu.SMEM),
                  pl.BlockSpec((1, 1, blk), lambda b, f, be, nu: (jnp.minimum(b + 1, nblk - 1), 0, 0),
                               memory_space=pltpu.SMEM),
                  pl.BlockSpec((1, blk, 1), lambda b, f, be, nu: (b, 0, 0)),
                  pl.BlockSpec(memory_space=pl.ANY),
                  pl.BlockSpec((1, D, tf), wmap_in),
                  pl.BlockSpec((1, D, tf), wmap_in),
                  pl.BlockSpec((1, tf, D), wmap_out)],
        out_specs=pl.BlockSpec((blk, D), lambda b, f, be, nu: (b, 0)),
        scratch_shapes=[pltpu.VMEM((2, blk, D), F32), pltpu.VMEM((blk, D), BF16),
                        pltpu.SemaphoreType.DMA((2,))],
    )
    return pl.pallas_call(
        functools.partial(_expert_kernel, blk=blk, nf=nf),
        grid_spec=grid_spec,
        out_shape=jax.ShapeDtypeStruct((nblk * blk, D), F32),
        compiler_params=_cparams(("arbitrary", "arbitrary")),
        name="experts",
    )(blk_e, nused, slot_tok, slot_tok, slot_w, xn2, wg, wu, wd)


def _combine_kernel(d_ref, dn_ref, h_ref, ys_hbm, o_ref, buf, sem, *, tm, nt):
    i = pl.program_id(0)
    slot = i % 2

    @pl.when(i == 0)
    def _():
        _gather_start(d_ref, ys_hbm, buf.at[0], sem.at[0], 2 * tm)

    _gather_wait(ys_hbm, buf.at[slot], sem.at[slot], 2 * tm)

    @pl.when(i + 1 < nt)
    def _():
        _gather_start(dn_ref, ys_hbm, buf.at[1 - slot], sem.at[1 - slot], 2 * tm)

    o_ref[...] = h_ref[...] + buf[slot, :tm, :] + buf[slot, tm:, :]


def _combine(h, ys, dest, tm=256):
    T, D = h.shape
    tm = min(tm, T)
    nt = T // tm
    d3 = dest.reshape(nt, tm, 2).transpose(0, 2, 1).reshape(nt, 1, 2 * tm)
    return pl.pallas_call(
        functools.partial(_combine_kernel, tm=tm, nt=nt),
        grid=(nt,),
        in_specs=[pl.BlockSpec((1, 1, 2 * tm), lambda i: (i, 0, 0), memory_space=pltpu.SMEM),
                  pl.BlockSpec((1, 1, 2 * tm), lambda i: (jnp.minimum(i + 1, nt - 1), 0, 0),
                               memory_space=pltpu.SMEM),
                  pl.BlockSpec((tm, D), lambda i: (i, 0)),
                  pl.BlockSpec(memory_space=pl.ANY)],
        out_specs=pl.BlockSpec((tm, D), lambda i: (i, 0)),
        out_shape=jax.ShapeDtypeStruct((T, D), F32),
        scratch_shapes=[pltpu.VMEM((2, 2 * tm, D), F32), pltpu.SemaphoreType.DMA((2,))],
        compiler_params=_cparams(("arbitrary",)),
        name="moe_combine",
    )(d3, d3, h, ys)


def _route(logits, blk):
    T = logits.shape[0]
    ng, epg, ne = N_EXPERT_GROUPS, EXPERTS_PER_GROUP, N_EXPERTS
    gp = jax.nn.softmax(logits[:, :ng], axis=-1)
    gsel = jnp.argmax(gp, axis=-1).astype(jnp.int32)
    gprob = jnp.take_along_axis(gp, gsel[:, None], axis=1)[:, 0]
    elog = logits[:, ng:ng + ne].reshape(T, ng, epg)
    elog = jnp.take_along_axis(elog, gsel[:, None, None], axis=1)[:, 0]
    topv, topi = lax.top_k(jax.nn.softmax(elog, axis=-1), TOP_K_INNER)
    gate = topv / jnp.sum(topv, axis=-1, keepdims=True) * gprob[:, None]
    eid = (gsel[:, None] * epg + topi.astype(jnp.int32)).reshape(-1)
    N = T * TOP_K_INNER
    onehot = (eid[:, None] == jnp.arange(ne, dtype=jnp.int32)[None, :]).astype(jnp.int32)
    csum = jnp.cumsum(onehot, axis=0)
    rank = jnp.take_along_axis(csum, eid[:, None], axis=1)[:, 0] - 1
    counts = csum[-1]
    pcounts = ((counts + blk - 1) // blk) * blk
    pends = jnp.cumsum(pcounts)
    pstarts = pends - pcounts
    dest = pstarts[eid] + rank
    nblk = N // blk + ne
    tok = jnp.arange(N, dtype=jnp.int32) // TOP_K_INNER
    slot_tok = jnp.zeros((nblk * blk,), jnp.int32).at[dest].set(tok)
    slot_w = jnp.zeros((nblk * blk,), F32).at[dest].set(gate.reshape(-1))
    blk_e = jnp.clip(jnp.searchsorted(pends, jnp.arange(nblk, dtype=jnp.int32) * blk, side='right'),
                     0, ne - 1).astype(jnp.int32)
    nused = (pends[-1] // blk).astype(jnp.int32).reshape(1)
    return (blk_e, nused, slot_tok.reshape(nblk, 1, blk), slot_w.reshape(nblk, blk, 1),
            dest.reshape(T, TOP_K_INNER).astype(jnp.int32))


def _moe(h, norm2_g, rw, rb, wg, wu, wd, blk=256):
    xn2, logits = _norm_router(h, norm2_g, rw, rb)
    blk = min(blk, h.shape[0])
    blk_e, nused, slot_tok, slot_w, dest = _route(logits, blk)
    ys = _experts(xn2, blk_e, nused, slot_tok, slot_w, wg, wu, wd, blk)
    return _combine(h, ys, dest)


def _layer(x3, lam_init, p):
    nb, L, D = x3.shape
    T = nb * L
    x = x3.reshape(T, D)
    w_in = p['w_in']
    att_w = N_HEADS * DV
    xn = _rmsnorm_cast(x, p['norm1_g'])
    qk = _matmul(xn, w_in, 0, 2 * att_w, F32, "in_proj_qk")
    vu = _matmul(xn, w_in, 2 * att_w, w_in.shape[1] - 2 * att_w, BF16, "in_proj_vu")
    qn, kn = _qk_prep(qk, p['q_norm_g'], p['k_norm_g'])
    att = _attention(qn, kn, vu, p['lam'], p['attn_sub_g'], lam_init, nb, L)
    y = _s5(vu[:, att_w:], p['s5'](L // S5_CHUNK), nb, L)
    ssm = _glu_norm(y, p['glu_w'], p['glu_b'], p['ssm_out_g'])
    h = _out_proj(att, ssm, p['w_out'], x)
    out = _moe(h, p['norm2_g'], p['rw'], p['rb'], p['wg'], p['wu'], p['wd'])
    return out.reshape(nb, L, D)


def kernel(x_prompt, x_sample, norm1_g, w_in, q_norm_g, k_norm_g, lam_q1, lam_k1, lam_q2, lam_k2, attn_sub_g, ssm_A_re, ssm_A_im, ssm_log_dt, ssm_B_re, ssm_B_im, ssm_C_re, ssm_C_im, ssm_D, glu_w, glu_b, ssm_out_g, w_out, norm2_g, router_group_w, router_group_b, router_expert_w, router_expert_b, exp_w_gate, exp_w_up, exp_w_down):
    depth = w_in.shape[0]

    def layer_params(l):
        lam_init = 0.8 - 0.6 * math.exp(-0.3 * l)
        lam = (jnp.exp(jnp.sum(lam_q1[l].astype(F32) * lam_k1[l].astype(F32)))
               - jnp.exp(jnp.sum(lam_q2[l].astype(F32) * lam_k2[l].astype(F32))) + lam_init)
        D = w_in.shape[1]
        ng, ne = router_group_w.shape[2], router_expert_w.shape[2]
        rw = jnp.zeros((D, ROUTER_PAD), F32)
        rw = rw.at[:, :ng].set(router_group_w[l].astype(F32)).at[:, ng:ng + ne].set(router_expert_w[l].astype(F32))
        rb = jnp.zeros((1, ROUTER_PAD), F32)
        rb = rb.at[0, :ng].set(router_group_b[l].astype(F32)).at[0, ng:ng + ne].set(router_expert_b[l].astype(F32))
        s5_cache = {}

        def s5(nc):
            if nc not in s5_cache:
                s5_cache[nc] = _s5_prep(ssm_A_re[l], ssm_A_im[l], ssm_log_dt[l], ssm_B_re[l], ssm_B_im[l],
                                        ssm_C_re[l], ssm_C_im[l], ssm_D[l], nc)
            return s5_cache[nc]

        return lam_init, dict(
            norm1_g=norm1_g[l], w_in=w_in[l].astype(BF16), q_norm_g=q_norm_g[l], k_norm_g=k_norm_g[l], lam=lam,
            attn_sub_g=attn_sub_g[l], s5=s5, glu_w=glu_w[l].astype(BF16), glu_b=glu_b[l], ssm_out_g=ssm_out_g[l],
            w_out=w_out[l].astype(BF16), norm2_g=norm2_g[l], rw=rw, rb=rb,
            wg=exp_w_gate[l].astype(BF16), wu=exp_w_up[l].astype(BF16), wd=exp_w_down[l].astype(BF16))

    params = [layer_params(l) for l in range(depth)]

    def run(x):
        for lam_init, p in params:
            x = _layer(x, lam_init, p)
        return x

    return (run(x_prompt), run(x_sample))
```

```python
import functools
import math

import jax
import jax.numpy as jnp
from jax import lax
from jax.experimental import pallas as pl
from jax.experimental.pallas import tpu as pltpu

F32 = jnp.float32
BF16 = jnp.bfloat16

RMS_EPS = 1e-6
LOG2E = 1.4426950408889634
LANES = 128
VMEM_LIMIT = 56 * 1024 * 1024

N_HEADS = 16
DQK = 64
DV = 2 * DQK
SSM_CH = 16
SSM_STATE = 64
S5_CHUNK = 16
N_EXPERT_GROUPS = 4
EXPERTS_PER_GROUP = 8
N_EXPERTS = N_EXPERT_GROUPS * EXPERTS_PER_GROUP
TOP_K_INNER = 2
ROUTER_PAD = LANES


def _cparams(sem):
    return pltpu.CompilerParams(dimension_semantics=sem, vmem_limit_bytes=VMEM_LIMIT)


def _rmsnorm_kernel(x_ref, g_ref, o_ref):
    x = x_ref[...]
    ms = jnp.mean(x * x, axis=-1, keepdims=True)
    o_ref[...] = (x * lax.rsqrt(ms + RMS_EPS) * g_ref[...]).astype(o_ref.dtype)


def _rmsnorm_cast(x, g, tm=512):
    T, D = x.shape
    tm = min(tm, T)
    return pl.pallas_call(
        _rmsnorm_kernel,
        grid=(T // tm,),
        in_specs=[pl.BlockSpec((tm, D), lambda i: (i, 0)),
                  pl.BlockSpec((1, D), lambda i: (0, 0))],
        out_specs=pl.BlockSpec((tm, D), lambda i: (i, 0)),
        out_shape=jax.ShapeDtypeStruct((T, D), BF16),
        compiler_params=_cparams(("parallel",)),
        name="rmsnorm1",
    )(x, g.reshape(1, D).astype(F32))


def _mm_kernel(a_ref, b_ref, o_ref):
    o_ref[...] = jnp.dot(a_ref[...], b_ref[...], preferred_element_type=F32).astype(o_ref.dtype)


def _matmul(a, b, col0, ncols, out_dtype, name, tm=1024, tn=512):
    M, K = a.shape
    tm = min(tm, M)
    tn = min(tn, ncols)
    cb = col0 // tn
    return pl.pallas_call(
        _mm_kernel,
        grid=(M // tm, ncols // tn),
        in_specs=[pl.BlockSpec((tm, K), lambda i, j: (i, 0)),
                  pl.BlockSpec((K, tn), lambda i, j: (0, j + cb))],
        out_specs=pl.BlockSpec((tm, tn), lambda i, j: (i, j)),
        out_shape=jax.ShapeDtypeStruct((M, ncols), out_dtype),
        compiler_params=_cparams(("parallel", "arbitrary")),
        name=name,
    )(a, b)


def _mm_qknorm_kernel(a_ref, b_ref, g_ref, o_ref, *, tn):
    acc = jnp.dot(a_ref[...], b_ref[...], preferred_element_type=F32)
    lane = lax.broadcasted_iota(jnp.int32, (1, LANES), 1)
    left = lane < DQK
    for c in range(tn // LANES):
        x = acc[:, c * LANES:(c + 1) * LANES]
        x2 = x * x
        s_all = jnp.sum(x2, axis=-1, keepdims=True)
        s_left = jnp.sum(jnp.where(left, x2, 0.0), axis=-1, keepdims=True)
        ms = jnp.where(left, s_left, s_all - s_left) * (1.0 / DQK)
        o_ref[:, c * LANES:(c + 1) * LANES] = (
            x * lax.rsqrt(ms + RMS_EPS) * g_ref[:, c * LANES:(c + 1) * LANES]).astype(o_ref.dtype)


def _proj_qk(xn, w, gq, gk, ncols, tm=1024, tn=512):
    M, K = xn.shape
    tm = min(tm, M)
    reps = ncols // (2 * DQK)
    g = jnp.concatenate([jnp.tile(gq.astype(F32) * (DQK ** -0.5 * LOG2E), reps),
                         jnp.tile(gk.astype(F32), reps)]).reshape(1, ncols)
    return pl.pallas_call(
        functools.partial(_mm_qknorm_kernel, tn=tn),
        grid=(M // tm, ncols // tn),
        in_specs=[pl.BlockSpec((tm, K), lambda i, j: (i, 0)),
                  pl.BlockSpec((K, tn), lambda i, j: (0, j)),
                  pl.BlockSpec((1, tn), lambda i, j: (0, j))],
        out_specs=pl.BlockSpec((tm, tn), lambda i, j: (i, j)),
        out_shape=jax.ShapeDtypeStruct((M, ncols), BF16),
        compiler_params=_cparams(("parallel", "arbitrary")),
        name="in_proj_qk",
    )(xn, w, g)


POS_SPLIT = 16


def _attn_kernel(lam_ref, sl_ref, q_ref, k_ref, v_ref, g_ref, o_ref, acc_ref, m_ref, s_ref, rm_ref,
                 *, tq, tk, nk, out_scale):
    h = pl.program_id(1)
    qi = pl.program_id(2)
    q0 = qi * tq
    s0, s1, s2, slope = sl_ref[h, 0], sl_ref[h, 1], sl_ref[h, 2], sl_ref[h, 3]
    lam = lam_ref[0]
    q = q_ref[...]
    lane = lax.broadcasted_iota(jnp.int32, (1, LANES), 1)
    zero = jnp.zeros_like(q)
    qs = jnp.concatenate([jnp.where(lane < DQK, q, zero), jnp.where(lane >= DQK, q, zero)], axis=0)

    def pos_parts(rows):
        r = lax.broadcasted_iota(jnp.int32, (rows, 1), 0)
        return (r & ~(POS_SPLIT - 1)).astype(F32), (r & (POS_SPLIT - 1)).astype(F32)

    def by_lane(vals, rows):
        out = jnp.zeros((rows, LANES), F32)
        for j, val in enumerate(vals):
            out = jnp.where(lane == j, val, out)
        return out

    qa, qb = pos_parts(tq)
    ka, kb = pos_parts(tk)
    qaug = by_lane([s0, s1, s2, s0, s1, s2, qa, qa, qa, qb, qb, qb], tq)
    kaug = by_lane([ka, ka, ka, kb, kb, kb, -s0, -s1, -s2, -s0, -s1, -s2], tk).astype(BF16)
    qcat = jnp.concatenate([qs, jnp.concatenate([qaug, qaug], axis=0).astype(BF16)], axis=1)
    ones_col = jnp.broadcast_to(jnp.where(lane == 0, 1.0, 0.0).astype(BF16), (tk, LANES))
    nt = (((1,), (1,)), ((), ()))

    m_ref[...] = jnp.full(m_ref.shape, -1e30, F32)
    acc_ref[...] = jnp.zeros(acc_ref.shape, F32)

    def put_scores(slot, sp):
        s_ref[slot] = sp
        rm_ref[slot] = jnp.broadcast_to(jnp.max(sp, axis=-1, keepdims=True), rm_ref.shape[1:])

    def update(slot, c, start):
        m_old = m_ref[...]
        m_new = jnp.maximum(m_old, rm_ref[slot] + c)
        alpha = jnp.exp2(m_old - m_new)
        mu = m_new - c
        p = jnp.exp2(s_ref[slot] - jnp.concatenate([mu] * (tk // LANES), axis=1)).astype(BF16)
        vcat = jnp.concatenate([v_ref[pl.ds(start, tk), :], ones_col], axis=1)
        acc_ref[...] = (jnp.concatenate([alpha, alpha], axis=1) * acc_ref[...]
                        + jnp.dot(p, vcat, preferred_element_type=F32))
        m_ref[...] = m_new

    kd0 = q0 // tk
    kd1 = (q0 + tq - 1) // tk + 1

    def diag(ki, carry):
        start = pl.multiple_of(ki * tk, tk)
        rowpos = (q0 + lax.broadcasted_iota(jnp.int32, (tq, 1), 0)).astype(F32)
        colpos = (ki * tk + lax.broadcasted_iota(jnp.int32, (1, tk), 1)).astype(F32)
        bias = -slope * jnp.abs(rowpos - colpos)
        sp = lax.dot_general(qs, k_ref[pl.ds(start, tk), :], nt, preferred_element_type=F32)
        put_scores(0, sp + jnp.concatenate([bias, bias], axis=0))
        update(0, 0.0, start)
        return carry

    lax.fori_loop(kd0, kd1, diag, 0)

    n_left = kd0
    n_side = n_left + (nk - kd1)

    def tile_of(t):
        t = jnp.minimum(t, n_side - 1)
        return jnp.clip(jnp.where(t < n_left, t, kd1 + t - n_left), 0, nk - 1), t < n_left

    def scores(slot, t):
        ki, is_left = tile_of(t)
        start = pl.multiple_of(ki * tk, tk)
        kc = jnp.concatenate([k_ref[pl.ds(start, tk), :], jnp.where(is_left, kaug, -kaug)], axis=1)
        put_scores(slot, lax.dot_general(qcat, kc, nt, preferred_element_type=F32))

    def consume(slot, t):
        ki, is_left = tile_of(t)
        dist0 = jnp.where(is_left, q0 - ki * tk, ki * tk - q0)
        update(slot, -slope * lax.convert_element_type(dist0, F32), pl.multiple_of(ki * tk, tk))

    def pair(i, carry):
        scores(1, 2 * i + 1)
        consume(0, 2 * i)
        scores(0, 2 * i + 2)
        consume(1, 2 * i + 1)
        return carry

    scores(0, 0)
    lax.fori_loop(0, n_side // 2, pair, 0)

    @pl.when(n_side % 2 == 1)
    def _():
        consume(0, n_side - 1)

    acc = acc_ref[...]
    o = acc[:tq, :DV] / acc[:tq, DV:DV + 1] - lam * (acc[tq:, :DV] / acc[tq:, DV:DV + 1])
    ms = jnp.mean(o * o, axis=-1, keepdims=True)
    o_ref[...] = (o * lax.rsqrt(ms + RMS_EPS) * (g_ref[...] * out_scale)).astype(o_ref.dtype)


def _split3_bf16(x):
    p0 = x.astype(BF16).astype(F32)
    p1 = (x - p0).astype(BF16).astype(F32)
    p2 = (x - p0 - p1).astype(BF16).astype(F32)
    return p0, p1, p2


def _attention(qkn, vu, lam, sub_g, lam_init, nb, L, tq=512, tk=512):
    T = qkn.shape[0]
    tq = min(tq, L)
    tk = min(tk, L)
    nq = L // tq
    slopes = jnp.exp2(-8.0 * jnp.arange(1, N_HEADS + 1, dtype=F32) / N_HEADS) * LOG2E
    p0, p1, p2 = _split3_bf16(slopes)
    sl = jnp.stack([p0, p1, p2, p0 + p1 + p2], axis=1)
    kern = functools.partial(_attn_kernel, tq=tq, tk=tk, nk=L // tk, out_scale=1.0 - lam_init)
    return pl.pallas_call(
        kern,
        grid=(nb, N_HEADS, nq),
        in_specs=[pl.BlockSpec(memory_space=pltpu.SMEM),
                  pl.BlockSpec(memory_space=pltpu.SMEM),
                  pl.BlockSpec((tq, LANES), lambda b, h, i: (b * nq + i, h)),
                  pl.BlockSpec((L, LANES), lambda b, h, i: (b, N_HEADS + h)),
                  pl.BlockSpec((L, LANES), lambda b, h, i: (b, h)),
                  pl.BlockSpec((1, DV), lambda b, h, i: (0, 0))],
        out_specs=pl.BlockSpec((tq, LANES), lambda b, h, i: (b * nq + i, h)),
        out_shape=jax.ShapeDtypeStruct((T, N_HEADS * DV), BF16),
        scratch_shapes=[pltpu.VMEM((2 * tq, 2 * LANES), F32), pltpu.VMEM((2 * tq, LANES), F32),
                        pltpu.VMEM((2, 2 * tq, tk), F32), pltpu.VMEM((2, 2 * tq, LANES), F32)],
        compiler_params=_cparams(("parallel", "parallel", "arbitrary")),
        name="diff_attn",
    )(lam.reshape(1).astype(F32), sl, qkn, qkn, vu, sub_g.reshape(1, DV).astype(F32))


def _s5_prep(A_re, A_im, log_dt, B_re, B_im, C_re, C_im, d_skip, nc):
    tc = S5_CHUNK
    hp = lax.Precision.HIGHEST
    A_re, A_im, log_dt = A_re.astype(F32), A_im.astype(F32), log_dt.astype(F32)
    B_re, B_im, C_re, C_im = B_re.astype(F32), B_im.astype(F32), C_re.astype(F32), C_im.astype(F32)
    G = A_re.shape[1]
    dt = jnp.exp(log_dt)[..., None]
    ang = A_im * dt
    mag = jnp.exp(A_re * dt)
    lr, li = mag * jnp.cos(ang), mag * jnp.sin(ang)
    den = A_re * A_re + A_im * A_im
    cr = ((lr - 1.0) * A_re + li * A_im) / den
    ci = (li * A_re - (lr - 1.0) * A_im) / den
    bbr = cr[..., None] * B_re - ci[..., None] * B_im
    bbi = cr[..., None] * B_im + ci[..., None] * B_re

    def lam_pow(n):
        nn = n.astype(F32)[:, None, None, None]
        pm = jnp.exp(A_re * dt * nn)
        pa = ang * nn
        return pm * jnp.cos(pa), pm * jnp.sin(pa)

    pr, pi = lam_pow(jnp.arange(tc + 1))
    lbr = pr[..., None] * bbr - pi[..., None] * bbi
    lbi = pr[..., None] * bbi + pi[..., None] * bbr
    kk = (jnp.einsum('dghp,ndgpb->ndghb', C_re, lbr, precision=hp)
          - jnp.einsum('dghp,ndgpb->ndghb', C_im, lbi, precision=hp))
    j = jnp.arange(tc)[:, None]
    t = jnp.arange(tc)[None, :]
    kf = jnp.where((t >= j)[:, :, None, None, None], kk[jnp.clip(t - j, 0, tc), 0], 0.0)
    kb = jnp.where((j >= t)[:, :, None, None, None], kk[jnp.clip(j - t, 0, tc), 1], 0.0)
    eye_t = (j == t).astype(F32)[:, :, None, None, None]
    eye_c = jnp.eye(SSM_CH, dtype=F32)[None, None, None]
    dd = eye_t * eye_c * d_skip.astype(F32)[None, None, :, :, None]
    ty = (kf + kb + dd).transpose(2, 0, 4, 1, 3).reshape(G, tc * SSM_CH, tc * SSM_CH)

    def state_cols(x):
        return x.transpose(1, 0, 3, 2).reshape(G, tc * SSM_CH, SSM_STATE)

    nf = tc - 1 - jnp.arange(tc)
    nbk = jnp.arange(tc)
    w1 = jnp.concatenate([ty, state_cols(lbr[nf, 0]), state_cols(lbi[nf, 0]),
                          state_cols(lbr[nbk, 1]), state_cols(lbi[nbk, 1])], axis=-1)

    def carry_rows(n, d):
        a = C_re[d][None] * pr[n, d][:, :, None, :] - C_im[d][None] * pi[n, d][:, :, None, :]
        bco = C_re[d][None] * pi[n, d][:, :, None, :] + C_im[d][None] * pr[n, d][:, :, None, :]
        to_rows = lambda x: x.transpose(1, 3, 0, 2).reshape(G, SSM_STATE, tc * SSM_CH)
        return to_rows(a), to_rows(-bco)

    fa, fb = carry_rows(jnp.arange(tc) + 1, 0)
    ba, bb = carry_rows(tc - jnp.arange(tc), 1)
    w2 = jnp.concatenate([fa, fb, ba, bb], axis=1)

    nsteps = max(1, int(math.ceil(math.log2(nc))))
    sr, si = lam_pow(tc * (2 ** jnp.arange(nsteps)))
    rows = jnp.stack([jnp.concatenate([sr, sr], -1), jnp.concatenate([-si, si], -1)], axis=2)
    coef = rows.transpose(3, 0, 1, 2, 4).reshape(G, nsteps * 4, LANES)
    return w1.astype(BF16), w2.astype(BF16), coef, nsteps


def _s5_kernel(u_ref, w1_ref, w2_ref, c_ref, o_ref, *, nc, nsteps):
    ny = S5_CHUNK * SSM_CH
    r = jnp.dot(u_ref[0], w1_ref[0], preferred_element_type=F32)
    y = r[:, :ny]
    row = lax.broadcasted_iota(jnp.int32, (nc, 1), 0)

    def shifted(x, d, fwd):
        if fwd:
            return jnp.where(row >= d, pltpu.roll(x, d, axis=0), 0.0)
        return jnp.where(row < nc - d, pltpu.roll(x, nc - d, axis=0), 0.0)

    def carry_in(x, dirn):
        s = shifted(x, 1, dirn == 0)
        for st in range(nsteps):
            d = 2 ** st
            if d >= nc:
                break
            sh = shifted(s, d, dirn == 0)
            a = c_ref[0, 4 * st + 2 * dirn:4 * st + 2 * dirn + 1, :]
            b = c_ref[0, 4 * st + 2 * dirn + 1:4 * st + 2 * dirn + 2, :]
            s = s + a * sh + b * pltpu.roll(sh, SSM_STATE, axis=1)
        return s

    sf = carry_in(r[:, ny:ny + LANES], 0)
    sb = carry_in(r[:, ny + LANES:ny + 2 * LANES], 1)
    s = jnp.concatenate([sf, sb], axis=1).astype(BF16)
    o_ref[0] = y + jnp.dot(s, w2_ref[0], preferred_element_type=F32)


def _s5(u, prep, nb, L):
    w1, w2, coef, nsteps = prep
    T, W = u.shape
    G = W // SSM_CH
    tc = S5_CHUNK
    nc = L // tc
    ny = tc * SSM_CH
    ug = u.reshape(nb * nc, tc, G, SSM_CH).transpose(2, 0, 1, 3).reshape(G, nb * nc, ny)
    yg = pl.pallas_call(
        functools.partial(_s5_kernel, nc=nc, nsteps=nsteps),
        grid=(nb, G),
        in_specs=[pl.BlockSpec((1, nc, ny), lambda b, g: (g, b, 0)),
                  pl.BlockSpec((1, ny, 2 * ny), lambda b, g: (g, 0, 0)),
                  pl.BlockSpec((1, ny, ny), lambda b, g: (g, 0, 0)),
                  pl.BlockSpec((1, 4 * nsteps, LANES), lambda b, g: (g, 0, 0))],
        out_specs=pl.BlockSpec((1, nc, ny), lambda b, g: (g, b, 0)),
        out_shape=jax.ShapeDtypeStruct((G, nb * nc, ny), F32),
        compiler_params=_cparams(("parallel", "parallel")),
        name="s5_chunked",
    )(ug, w1, w2, coef)
    return yg.reshape(G, nb * nc, tc, SSM_CH).transpose(1, 2, 0, 3).reshape(T, W)


def _glu_kernel(y_ref, w_ref, b_ref, g_ref, o_ref):
    y = y_ref[...]
    c0 = math.sqrt(2.0 / math.pi)
    gl = 0.5 * y * (1.0 + jnp.tanh(c0 * (y + 0.044715 * (y * y * y))))
    z = jnp.dot(gl.astype(BF16), w_ref[...], preferred_element_type=F32) + b_ref[...]
    s = gl * (1.0 / (1.0 + jnp.exp(-z)))
    ms = jnp.mean(s * s, axis=-1, keepdims=True)
    o_ref[...] = (s * lax.rsqrt(ms + RMS_EPS) * g_ref[...]).astype(o_ref.dtype)


def _glu_norm(y, w, b, g, tm=512):
    T, W = y.shape
    tm = min(tm, T)
    return pl.pallas_call(
        _glu_kernel,
        grid=(T // tm,),
        in_specs=[pl.BlockSpec((tm, W), lambda i: (i, 0)),
                  pl.BlockSpec((W, W), lambda i: (0, 0)),
                  pl.BlockSpec((1, W), lambda i: (0, 0)),
                  pl.BlockSpec((1, W), lambda i: (0, 0))],
        out_specs=pl.BlockSpec((tm, W), lambda i: (i, 0)),
        out_shape=jax.ShapeDtypeStruct((T, W), BF16),
        compiler_params=_cparams(("parallel",)),
        name="glu_norm",
    )(y, w, b.reshape(1, W).astype(F32), g.reshape(1, W).astype(F32))


def _out_proj_kernel(a_ref, s_ref, wa_ref, ws_ref, x_ref, o_ref):
    acc = jnp.dot(a_ref[...], wa_ref[...], preferred_element_type=F32)
    acc = acc + jnp.dot(s_ref[...], ws_ref[...], preferred_element_type=F32)
    o_ref[...] = x_ref[...] + acc


def _out_proj(att, ssm, w, x, tm=1024, tn=512):
    T, Wa = att.shape
    D = x.shape[1]
    tm = min(tm, T)
    tn = min(tn, D)
    rb = Wa // Wa
    return pl.pallas_call(
        _out_proj_kernel,
        grid=(T // tm, D // tn),
        in_specs=[pl.BlockSpec((tm, Wa), lambda i, j: (i, 0)),
                  pl.BlockSpec((tm, Wa), lambda i, j: (i, 0)),
                  pl.BlockSpec((Wa, tn), lambda i, j: (0, j)),
                  pl.BlockSpec((Wa, tn), lambda i, j: (rb, j)),
                  pl.BlockSpec((tm, tn), lambda i, j: (i, j))],
        out_specs=pl.BlockSpec((tm, tn), lambda i, j: (i, j)),
        out_shape=jax.ShapeDtypeStruct((T, D), F32),
        compiler_params=_cparams(("parallel", "arbitrary")),
        name="out_proj",
    )(att, ssm, w, w, x)


def _router_kernel(h_ref, g_ref, w_ref, b_ref, xn_ref, lg_ref):
    x = h_ref[...]
    ms = jnp.mean(x * x, axis=-1, keepdims=True)
    xn = x * lax.rsqrt(ms + RMS_EPS) * g_ref[...]
    xn_ref[...] = xn
    lg_ref[...] = jnp.dot(xn, w_ref[...], preferred_element_type=F32,
                          precision=lax.Precision.HIGHEST) + b_ref[...]


def _norm_router(h, g, rw, rb, tm=256):
    T, D = h.shape
    tm = min(tm, T)
    return pl.pallas_call(
        _router_kernel,
        grid=(T // tm,),
        in_specs=[pl.BlockSpec((tm, D), lambda i: (i, 0)),
                  pl.BlockSpec((1, D), lambda i: (0, 0)),
                  pl.BlockSpec((D, ROUTER_PAD), lambda i: (0, 0)),
                  pl.BlockSpec((1, ROUTER_PAD), lambda i: (0, 0))],
        out_specs=[pl.BlockSpec((tm, D), lambda i: (i, 0)),
                   pl.BlockSpec((tm, ROUTER_PAD), lambda i: (i, 0))],
        out_shape=[jax.ShapeDtypeStruct((T, D), F32), jax.ShapeDtypeStruct((T, ROUTER_PAD), F32)],
        compiler_params=_cparams(("parallel",)),
        name="norm_router",
    )(h, g.reshape(1, D).astype(F32), rw, rb)


def _row_copy(src_hbm, src_row, dst, dst_row, sem):
    return pltpu.make_async_copy(src_hbm.at[pl.ds(src_row, 1), :], dst.at[pl.ds(dst_row, 1), :], sem)


def _gather_start(idx_ref, src_hbm, dst, sem, n):
    def body(r, c):
        _row_copy(src_hbm, idx_ref[0, 0, r], dst, r, sem).start()
        return c
    lax.fori_loop(0, n, body, 0)


def _gather_wait(src_hbm, dst, sem, n):
    def body(r, c):
        _row_copy(src_hbm, 0, dst, r, sem).wait()
        return c
    lax.fori_loop(0, n, body, 0)


def _expert_up_kernel(be_ref, nu_ref, tok_ref, tokn_ref, x_hbm, wg_ref, wu_ref, h_ref, xbuf, sem, *, blk):
    b = pl.program_id(0)
    nused = nu_ref[0]
    slot = b % 2

    @pl.when(b == 0)
    def _():
        _gather_start(tok_ref, x_hbm, xbuf.at[0], sem.at[0], blk)

    @pl.when(b < nused)
    def _():
        _gather_wait(x_hbm, xbuf.at[slot], sem.at[slot], blk)

    @pl.when(b + 1 < nused)
    def _():
        _gather_start(tokn_ref, x_hbm, xbuf.at[1 - slot], sem.at[1 - slot], blk)

    @pl.when(b < nused)
    def _():
        x = xbuf[slot].astype(BF16)
        g = jnp.dot(x, wg_ref[0], preferred_element_type=F32)
        u = jnp.dot(x, wu_ref[0], preferred_element_type=F32)
        h_ref[...] = (g * (1.0 / (1.0 + jnp.exp(-g))) * u).astype(h_ref.dtype)

    @pl.when(b >= nused)
    def _():
        h_ref[...] = jnp.zeros_like(h_ref)


def _expert_down_kernel(be_ref, nu_ref, h_ref, sw_ref, wd_ref, o_ref):
    live = pl.program_id(0) < nu_ref[0]

    @pl.when(live)
    def _():
        o_ref[...] = jnp.dot(h_ref[...], wd_ref[0], preferred_element_type=F32) * sw_ref[0]

    @pl.when(jnp.logical_not(live))
    def _():
        o_ref[...] = jnp.zeros_like(o_ref)


def _experts(xn2, blk_e, nused, slot_tok, slot_w, wg, wu, wd, blk):
    T, D = xn2.shape
    nblk = slot_tok.shape[0]
    dff = wg.shape[2]

    def wmap(b, be, nu):
        return (be[jnp.minimum(b, nu[0] - 1)], 0, 0)

    hid = pl.pallas_call(
        functools.partial(_expert_up_kernel, blk=blk),
        grid_spec=pltpu.PrefetchScalarGridSpec(
            num_scalar_prefetch=2,
            grid=(nblk,),
            in_specs=[pl.BlockSpec((1, 1, blk), lambda b, be, nu: (b, 0, 0), memory_space=pltpu.SMEM),
                      pl.BlockSpec((1, 1, blk), lambda b, be, nu: (jnp.minimum(b + 1, nblk - 1), 0, 0),
                                   memory_space=pltpu.SMEM),
                      pl.BlockSpec(memory_space=pl.ANY),
                      pl.BlockSpec((1, D, dff), wmap),
                      pl.BlockSpec((1, D, dff), wmap)],
            out_specs=pl.BlockSpec((blk, dff), lambda b, be, nu: (b, 0)),
            scratch_shapes=[pltpu.VMEM((2, blk, D), F32), pltpu.SemaphoreType.DMA((2,))],
        ),
        out_shape=jax.ShapeDtypeStruct((nblk * blk, dff), BF16),
        compiler_params=_cparams(("arbitrary",)),
        name="experts_up",
    )(blk_e, nused, slot_tok, slot_tok, xn2, wg, wu)
    return pl.pallas_call(
        _expert_down_kernel,
        grid_spec=pltpu.PrefetchScalarGridSpec(
            num_scalar_prefetch=2,
            grid=(nblk,),
            in_specs=[pl.BlockSpec((blk, dff), lambda b, be, nu: (b, 0)),
                      pl.BlockSpec((1, blk, 1), lambda b, be, nu: (b, 0, 0)),
                      pl.BlockSpec((1, dff, D), wmap)],
            out_specs=pl.BlockSpec((blk, D), lambda b, be, nu: (b, 0)),
        ),
        out_shape=jax.ShapeDtypeStruct((nblk * blk, D), F32),
        compiler_params=_cparams(("arbitrary",)),
        name="experts_down",
    )(blk_e, nused, hid, slot_w, wd)


def _combine_kernel(d_ref, dn_ref, h_ref, ys_hbm, o_ref, buf, sem, *, tm, nt):
    i = pl.program_id(0)
    slot = i % 2

    @pl.when(i == 0)
    def _():
        _gather_start(d_ref, ys_hbm, buf.at[0], sem.at[0], 2 * tm)

    _gather_wait(ys_hbm, buf.at[slot], sem.at[slot], 2 * tm)

    @pl.when(i + 1 < nt)
    def _():
        _gather_start(dn_ref, ys_hbm, buf.at[1 - slot], sem.at[1 - slot], 2 * tm)

    o_ref[...] = h_ref[...] + buf[slot, :tm, :] + buf[slot, tm:, :]


def _combine(h, ys, dest, tm=256):
    T, D = h.shape
    tm = min(tm, T)
    nt = T // tm
    d3 = dest.reshape(nt, tm, 2).transpose(0, 2, 1).reshape(nt, 1, 2 * tm)
    return pl.pallas_call(
        functools.partial(_combine_kernel, tm=tm, nt=nt),
        grid=(nt,),
        in_specs=[pl.BlockSpec((1, 1, 2 * tm), lambda i: (i, 0, 0), memory_space=pltpu.SMEM),
                  pl.BlockSpec((1, 1, 2 * tm), lambda i: (jnp.minimum(i + 1, nt - 1), 0, 0),
                               memory_space=pltpu.SMEM),
                  pl.BlockSpec((tm, D), lambda i: (i, 0)),
                  pl.BlockSpec(memory_space=pl.ANY)],
        out_specs=pl.BlockSpec((tm, D), lambda i: (i, 0)),
        out_shape=jax.ShapeDtypeStruct((T, D), F32),
        scratch_shapes=[pltpu.VMEM((2, 2 * tm, D), F32), pltpu.SemaphoreType.DMA((2,))],
        compiler_params=_cparams(("arbitrary",)),
        name="moe_combine",
    )(d3, d3, h, ys)


def _route(logits, blk):
    T = logits.shape[0]
    ng, epg, ne = N_EXPERT_GROUPS, EXPERTS_PER_GROUP, N_EXPERTS
    gp = jax.nn.softmax(logits[:, :ng], axis=-1)
    gsel = jnp.argmax(gp, axis=-1).astype(jnp.int32)
    gprob = jnp.take_along_axis(gp, gsel[:, None], axis=1)[:, 0]
    elog = logits[:, ng:ng + ne].reshape(T, ng, epg)
    elog = jnp.take_along_axis(elog, gsel[:, None, None], axis=1)[:, 0]
    topv, topi = lax.top_k(jax.nn.softmax(elog, axis=-1), TOP_K_INNER)
    gate = topv / jnp.sum(topv, axis=-1, keepdims=True) * gprob[:, None]
    eid = (gsel[:, None] * epg + topi.astype(jnp.int32)).reshape(-1)
    N = T * TOP_K_INNER
    onehot = (eid[:, None] == jnp.arange(ne, dtype=jnp.int32)[None, :]).astype(jnp.int32)
    csum = jnp.cumsum(onehot, axis=0)
    rank = jnp.take_along_axis(csum, eid[:, None], axis=1)[:, 0] - 1
    counts = csum[-1]
    pcounts = ((counts + blk - 1) // blk) * blk
    pends = jnp.cumsum(pcounts)
    pstarts = pends - pcounts
    dest = pstarts[eid] + rank
    nblk = N // blk + ne
    tok = jnp.arange(N, dtype=jnp.int32) // TOP_K_INNER
    slot_tok = jnp.zeros((nblk * blk,), jnp.int32).at[dest].set(tok)
    slot_w = jnp.zeros((nblk * blk,), F32).at[dest].set(gate.reshape(-1))
    blk_e = jnp.clip(jnp.searchsorted(pends, jnp.arange(nblk, dtype=jnp.int32) * blk, side='right'),
                     0, ne - 1).astype(jnp.int32)
    nused = (pends[-1] // blk).astype(jnp.int32).reshape(1)
    return (blk_e, nused, slot_tok.reshape(nblk, 1, blk), slot_w.reshape(nblk, blk, 1),
            dest.reshape(T, TOP_K_INNER).astype(jnp.int32))


def _moe(h, norm2_g, rw, rb, wg, wu, wd, blk=256):
    xn2, logits = _norm_router(h, norm2_g, rw, rb)
    blk = min(blk, h.shape[0])
    blk_e, nused, slot_tok, slot_w, dest = _route(logits, blk)
    ys = _experts(xn2, blk_e, nused, slot_tok, slot_w, wg, wu, wd, blk)
    return _combine(h, ys, dest)


def _layer(x3, lam_init, p):
    nb, L, D = x3.shape
    T = nb * L
    x = x3.reshape(T, D)
    w_in = p['w_in']
    att_w = N_HEADS * DV
    xn = _rmsnorm_cast(x, p['norm1_g'])
    qkn = _proj_qk(xn, w_in, p['q_norm_g'], p['k_norm_g'], 2 * att_w)
    vu = _matmul(xn, w_in, 2 * att_w, w_in.shape[1] - 2 * att_w, BF16, "in_proj_vu")
    att = _attention(qkn, vu, p['lam'], p['attn_sub_g'], lam_init, nb, L)
    y = _s5(vu[:, att_w:], p['s5'](L // S5_CHUNK), nb, L)
    ssm = _glu_norm(y, p['glu_w'], p['glu_b'], p['ssm_out_g'])
    h = _out_proj(att, ssm, p['w_out'], x)
    out = _moe(h, p['norm2_g'], p['rw'], p['rb'], p['wg'], p['wu'], p['wd'])
    return out.reshape(nb, L, D)


def kernel(x_prompt, x_sample, norm1_g, w_in, q_norm_g, k_norm_g, lam_q1, lam_k1, lam_q2, lam_k2, attn_sub_g, ssm_A_re, ssm_A_im, ssm_log_dt, ssm_B_re, ssm_B_im, ssm_C_re, ssm_C_im, ssm_D, glu_w, glu_b, ssm_out_g, w_out, norm2_g, router_group_w, router_group_b, router_expert_w, router_expert_b, exp_w_gate, exp_w_up, exp_w_down):
    depth = w_in.shape[0]

    def layer_params(l):
        lam_init = 0.8 - 0.6 * math.exp(-0.3 * l)
        lam = (jnp.exp(jnp.sum(lam_q1[l].astype(F32) * lam_k1[l].astype(F32)))
               - jnp.exp(jnp.sum(lam_q2[l].astype(F32) * lam_k2[l].astype(F32))) + lam_init)
        D = w_in.shape[1]
        ng, ne = router_group_w.shape[2], router_expert_w.shape[2]
        rw = jnp.zeros((D, ROUTER_PAD), F32)
        rw = rw.at[:, :ng].set(router_group_w[l].astype(F32)).at[:, ng:ng + ne].set(router_expert_w[l].astype(F32))
        rb = jnp.zeros((1, ROUTER_PAD), F32)
        rb = rb.at[0, :ng].set(router_group_b[l].astype(F32)).at[0, ng:ng + ne].set(router_expert_b[l].astype(F32))
        s5_cache = {}

        def s5(nc):
            if nc not in s5_cache:
                s5_cache[nc] = _s5_prep(ssm_A_re[l], ssm_A_im[l], ssm_log_dt[l], ssm_B_re[l], ssm_B_im[l],
                                        ssm_C_re[l], ssm_C_im[l], ssm_D[l], nc)
            return s5_cache[nc]

        return lam_init, dict(
            norm1_g=norm1_g[l], w_in=w_in[l].astype(BF16), q_norm_g=q_norm_g[l], k_norm_g=k_norm_g[l], lam=lam,
            attn_sub_g=attn_sub_g[l], s5=s5, glu_w=glu_w[l].astype(BF16), glu_b=glu_b[l], ssm_out_g=ssm_out_g[l],
            w_out=w_out[l].astype(BF16), norm2_g=norm2_g[l], rw=rw, rb=rb,
            wg=exp_w_gate[l].astype(BF16), wu=exp_w_up[l].astype(BF16), wd=exp_w_down[l].astype(BF16))

    params = [layer_params(l) for l in range(depth)]

    def run(x):
        for lam_init, p in params:
            x = _layer(x, lam_init, p)
        return x

    return (run(x_prompt), run(x_sample))
```

```python
import functools
import math

import jax
import jax.numpy as jnp
from jax import lax
from jax.experimental import pallas as pl
from jax.experimental.pallas import tpu as pltpu

F32 = jnp.float32
BF16 = jnp.bfloat16

RMS_EPS = 1e-6
LOG2E = 1.4426950408889634
LANES = 128
VMEM_LIMIT = 56 * 1024 * 1024

N_HEADS = 16
DQK = 64
DV = 2 * DQK
SSM_CH = 16
SSM_STATE = 64
S5_CHUNK = 16
N_EXPERT_GROUPS = 4
EXPERTS_PER_GROUP = 8
N_EXPERTS = N_EXPERT_GROUPS * EXPERTS_PER_GROUP
TOP_K_INNER = 2
ROUTER_PAD = LANES


def _cparams(sem):
    return pltpu.CompilerParams(dimension_semantics=sem, vmem_limit_bytes=VMEM_LIMIT)


def _rmsnorm_kernel(x_ref, g_ref, o_ref):
    x = x_ref[...]
    ms = jnp.mean(x * x, axis=-1, keepdims=True)
    o_ref[...] = (x * lax.rsqrt(ms + RMS_EPS) * g_ref[...]).astype(o_ref.dtype)


def _rmsnorm_cast(x, g, tm=512):
    T, D = x.shape
    tm = min(tm, T)
    return pl.pallas_call(
        _rmsnorm_kernel,
        grid=(T // tm,),
        in_specs=[pl.BlockSpec((tm, D), lambda i: (i, 0)),
                  pl.BlockSpec((1, D), lambda i: (0, 0))],
        out_specs=pl.BlockSpec((tm, D), lambda i: (i, 0)),
        out_shape=jax.ShapeDtypeStruct((T, D), BF16),
        compiler_params=_cparams(("parallel",)),
        name="rmsnorm1",
    )(x, g.reshape(1, D).astype(F32))


def _mm_kernel(a_ref, b_ref, o_ref):
    o_ref[...] = jnp.dot(a_ref[...], b_ref[...], preferred_element_type=F32).astype(o_ref.dtype)


def _matmul(a, b, col0, ncols, out_dtype, name, tm=1024, tn=512):
    M, K = a.shape
    tm = min(tm, M)
    tn = min(tn, ncols)
    cb = col0 // tn
    return pl.pallas_call(
        _mm_kernel,
        grid=(M // tm, ncols // tn),
        in_specs=[pl.BlockSpec((tm, K), lambda i, j: (i, 0)),
                  pl.BlockSpec((K, tn), lambda i, j: (0, j + cb))],
        out_specs=pl.BlockSpec((tm, tn), lambda i, j: (i, j)),
        out_shape=jax.ShapeDtypeStruct((M, ncols), out_dtype),
        compiler_params=_cparams(("parallel", "arbitrary")),
        name=name,
    )(a, b)


def _mm_qknorm_kernel(a_ref, b_ref, g_ref, o_ref, *, tn):
    acc = jnp.dot(a_ref[...], b_ref[...], preferred_element_type=F32)
    lane = lax.broadcasted_iota(jnp.int32, (1, LANES), 1)
    left = lane < DQK
    for c in range(tn // LANES):
        x = acc[:, c * LANES:(c + 1) * LANES]
        x2 = x * x
        s_all = jnp.sum(x2, axis=-1, keepdims=True)
        s_left = jnp.sum(jnp.where(left, x2, 0.0), axis=-1, keepdims=True)
        ms = jnp.where(left, s_left, s_all - s_left) * (1.0 / DQK)
        o_ref[:, c * LANES:(c + 1) * LANES] = (
            x * lax.rsqrt(ms + RMS_EPS) * g_ref[:, c * LANES:(c + 1) * LANES]).astype(o_ref.dtype)


def _proj_qk(xn, w, gq, gk, ncols, tm=1024, tn=512):
    M, K = xn.shape
    tm = min(tm, M)
    reps = ncols // (2 * DQK)
    g = jnp.concatenate([jnp.tile(gq.astype(F32) * (DQK ** -0.5 * LOG2E), reps),
                         jnp.tile(gk.astype(F32), reps)]).reshape(1, ncols)
    return pl.pallas_call(
        functools.partial(_mm_qknorm_kernel, tn=tn),
        grid=(M // tm, ncols // tn),
        in_specs=[pl.BlockSpec((tm, K), lambda i, j: (i, 0)),
                  pl.BlockSpec((K, tn), lambda i, j: (0, j)),
                  pl.BlockSpec((1, tn), lambda i, j: (0, j))],
        out_specs=pl.BlockSpec((tm, tn), lambda i, j: (i, j)),
        out_shape=jax.ShapeDtypeStruct((M, ncols), BF16),
        compiler_params=_cparams(("parallel", "arbitrary")),
        name="in_proj_qk",
    )(xn, w, g)


POS_SPLIT = 16


def _attn_kernel(lam_ref, sl_ref, dist_ref, q_ref, k_ref, v_ref, g_ref, o_ref, acc_ref, m_ref, s_ref, rm_ref,
                 *, tq, tk, nk, out_scale):
    h = pl.program_id(1)
    qi = pl.program_id(2)
    q0 = qi * tq
    s0, s1, s2, slope = sl_ref[h, 0], sl_ref[h, 1], sl_ref[h, 2], sl_ref[h, 3]
    lam = lam_ref[0]
    q = q_ref[...]
    lane = lax.broadcasted_iota(jnp.int32, (1, LANES), 1)
    zero = jnp.zeros_like(q)
    qs = jnp.concatenate([jnp.where(lane < DQK, q, zero), jnp.where(lane >= DQK, q, zero)], axis=0)

    def pos_parts(rows):
        r = lax.broadcasted_iota(jnp.int32, (rows, 1), 0)
        return (r & ~(POS_SPLIT - 1)).astype(F32), (r & (POS_SPLIT - 1)).astype(F32)

    def by_lane(vals, rows):
        out = jnp.zeros((rows, LANES), F32)
        for j, val in enumerate(vals):
            out = jnp.where(lane == j, val, out)
        return out

    qa, qb = pos_parts(tq)
    ka, kb = pos_parts(tk)
    qaug = by_lane([s0, s1, s2, s0, s1, s2, qa, qa, qa, qb, qb, qb], tq)
    kaug = by_lane([ka, ka, ka, kb, kb, kb, -s0, -s1, -s2, -s0, -s1, -s2], tk).astype(BF16)
    qcat = jnp.concatenate([qs, jnp.concatenate([qaug, qaug], axis=0).astype(BF16)], axis=1)
    ones_col = jnp.broadcast_to(jnp.where(lane == 0, 1.0, 0.0).astype(BF16), (tk, LANES))
    nt = (((1,), (1,)), ((), ()))

    m_ref[...] = jnp.full(m_ref.shape, -1e30, F32)
    acc_ref[...] = jnp.zeros(acc_ref.shape, F32)

    def row_max(sp):
        return jnp.broadcast_to(jnp.max(sp, axis=-1, keepdims=True), (2 * tq, LANES))

    def update(sp, rm, c, start):
        m_old = m_ref[...]
        m_new = jnp.maximum(m_old, rm + c)
        alpha = jnp.exp2(m_old - m_new)
        mu = m_new - c
        p = jnp.exp2(sp - jnp.concatenate([mu] * (tk // LANES), axis=1)).astype(BF16)
        vcat = jnp.concatenate([v_ref[pl.ds(start, tk), :], ones_col], axis=1)
        acc_ref[...] = (jnp.concatenate([alpha, alpha], axis=1) * acc_ref[...]
                        + jnp.dot(p, vcat, preferred_element_type=F32))
        m_ref[...] = m_new

    dist = dist_ref[h]
    lo = jnp.maximum(q0 - dist, 0) // tk
    hi = jnp.minimum((q0 + tq - 1 + dist) // tk + 1, nk)
    n_left = qi - lo
    n_side = n_left + (hi - qi - 1)

    def tile_of(t):
        t = jnp.minimum(t, n_side - 1)
        return jnp.clip(jnp.where(t < n_left, lo + t, qi + 1 + t - n_left), 0, nk - 1), t < n_left

    def scores(slot, t):
        ki, is_left = tile_of(t)
        start = pl.multiple_of(ki * tk, tk)
        kc = jnp.concatenate([k_ref[pl.ds(start, tk), :], jnp.where(is_left, kaug, -kaug)], axis=1)
        sp = lax.dot_general(qcat, kc, nt, preferred_element_type=F32)
        s_ref[slot] = sp
        rm_ref[slot] = row_max(sp)

    def consume(slot, t):
        ki, is_left = tile_of(t)
        dist0 = jnp.where(is_left, q0 - ki * tk, ki * tk - q0)
        update(s_ref[slot], rm_ref[slot], -slope * lax.convert_element_type(dist0, F32),
               pl.multiple_of(ki * tk, tk))

    def pair(i, carry):
        scores(1, 2 * i + 1)
        consume(0, 2 * i)
        scores(0, 2 * i + 2)
        consume(1, 2 * i + 1)
        return carry

    scores(0, 0)
    start_d = pl.multiple_of(q0, tk)
    rel = (lax.broadcasted_iota(jnp.int32, (tq, 1), 0) - lax.broadcasted_iota(jnp.int32, (1, tk), 1)).astype(F32)
    bias = -slope * jnp.abs(rel)
    sp_d = (lax.dot_general(qs, k_ref[pl.ds(start_d, tk), :], nt, preferred_element_type=F32)
            + jnp.concatenate([bias, bias], axis=0))
    update(sp_d, row_max(sp_d), 0.0, start_d)
    lax.fori_loop(0, n_side // 2, pair, 0)

    @pl.when(n_side % 2 == 1)
    def _():
        consume(0, n_side - 1)

    acc = acc_ref[...]
    o = acc[:tq, :DV] / acc[:tq, DV:DV + 1] - lam * (acc[tq:, :DV] / acc[tq:, DV:DV + 1])
    ms = jnp.mean(o * o, axis=-1, keepdims=True)
    o_ref[...] = (o * lax.rsqrt(ms + RMS_EPS) * (g_ref[...] * out_scale)).astype(o_ref.dtype)


def _split3_bf16(x):
    p0 = x.astype(BF16).astype(F32)
    p1 = (x - p0).astype(BF16).astype(F32)
    p2 = (x - p0 - p1).astype(BF16).astype(F32)
    return p0, p1, p2


F32_EXP_ZERO = 104.0


def _zero_weight_distance(gq, gk, slopes_nat, L):
    bound = 8.0 * jnp.max(jnp.abs(gq.astype(F32))) * jnp.max(jnp.abs(gk.astype(F32))) * 1.02
    d = jnp.ceil((F32_EXP_ZERO + 2.0 * bound) / slopes_nat) + 1.0
    return jnp.clip(d, 1.0, float(L)).astype(jnp.int32)


def _attention(qkn, vu, lam, sub_g, gq, gk, lam_init, nb, L, tile=512):
    T = qkn.shape[0]
    tq = tk = min(tile, L)
    nq = L // tq
    slopes_nat = jnp.exp2(-8.0 * jnp.arange(1, N_HEADS + 1, dtype=F32) / N_HEADS)
    p0, p1, p2 = _split3_bf16(slopes_nat * LOG2E)
    sl = jnp.stack([p0, p1, p2, p0 + p1 + p2], axis=1)
    dist = _zero_weight_distance(gq, gk, slopes_nat, L)
    kern = functools.partial(_attn_kernel, tq=tq, tk=tk, nk=L // tk, out_scale=1.0 - lam_init)
    return pl.pallas_call(
        kern,
        grid=(nb, N_HEADS, nq),
        in_specs=[pl.BlockSpec(memory_space=pltpu.SMEM),
                  pl.BlockSpec(memory_space=pltpu.SMEM),
                  pl.BlockSpec(memory_space=pltpu.SMEM),
                  pl.BlockSpec((tq, LANES), lambda b, h, i: (b * nq + i, h)),
                  pl.BlockSpec((L, LANES), lambda b, h, i: (b, N_HEADS + h)),
                  pl.BlockSpec((L, LANES), lambda b, h, i: (b, h)),
                  pl.BlockSpec((1, DV), lambda b, h, i: (0, 0))],
        out_specs=pl.BlockSpec((tq, LANES), lambda b, h, i: (b * nq + i, h)),
        out_shape=jax.ShapeDtypeStruct((T, N_HEADS * DV), BF16),
        scratch_shapes=[pltpu.VMEM((2 * tq, 2 * LANES), F32), pltpu.VMEM((2 * tq, LANES), F32),
                        pltpu.VMEM((2, 2 * tq, tk), F32), pltpu.VMEM((2, 2 * tq, LANES), F32)],
        compiler_params=_cparams(("parallel", "parallel", "arbitrary")),
        name="diff_attn",
    )(lam.reshape(1).astype(F32), sl, dist, qkn, qkn, vu, sub_g.reshape(1, DV).astype(F32))


def _s5_prep(A_re, A_im, log_dt, B_re, B_im, C_re, C_im, d_skip, nc):
    tc = S5_CHUNK
    hp = lax.Precision.HIGHEST
    A_re, A_im, log_dt = A_re.astype(F32), A_im.astype(F32), log_dt.astype(F32)
    B_re, B_im, C_re, C_im = B_re.astype(F32), B_im.astype(F32), C_re.astype(F32), C_im.astype(F32)
    G = A_re.shape[1]
    dt = jnp.exp(log_dt)[..., None]
    ang = A_im * dt
    mag = jnp.exp(A_re * dt)
    lr, li = mag * jnp.cos(ang), mag * jnp.sin(ang)
    den = A_re * A_re + A_im * A_im
    cr = ((lr - 1.0) * A_re + li * A_im) / den
    ci = (li * A_re - (lr - 1.0) * A_im) / den
    bbr = cr[..., None] * B_re - ci[..., None] * B_im
    bbi = cr[..., None] * B_im + ci[..., None] * B_re

    def lam_pow(n):
        nn = n.astype(F32)[:, None, None, None]
        pm = jnp.exp(A_re * dt * nn)
        pa = ang * nn
        return pm * jnp.cos(pa), pm * jnp.sin(pa)

    pr, pi = lam_pow(jnp.arange(tc + 1))
    lbr = pr[..., None] * bbr - pi[..., None] * bbi
    lbi = pr[..., None] * bbi + pi[..., None] * bbr
    kk = (jnp.einsum('dghp,ndgpb->ndghb', C_re, lbr, precision=hp)
          - jnp.einsum('dghp,ndgpb->ndghb', C_im, lbi, precision=hp))
    j = jnp.arange(tc)[:, None]
    t = jnp.arange(tc)[None, :]
    kf = jnp.where((t >= j)[:, :, None, None, None], kk[jnp.clip(t - j, 0, tc), 0], 0.0)
    kb = jnp.where((j >= t)[:, :, None, None, None], kk[jnp.clip(j - t, 0, tc), 1], 0.0)
    eye_t = (j == t).astype(F32)[:, :, None, None, None]
    eye_c = jnp.eye(SSM_CH, dtype=F32)[None, None, None]
    dd = eye_t * eye_c * d_skip.astype(F32)[None, None, :, :, None]
    ty = (kf + kb + dd).transpose(2, 0, 4, 1, 3).reshape(G, tc * SSM_CH, tc * SSM_CH)

    def state_cols(x):
        return x.transpose(1, 0, 3, 2).reshape(G, tc * SSM_CH, SSM_STATE)

    nf = tc - 1 - jnp.arange(tc)
    nbk = jnp.arange(tc)
    w1 = jnp.concatenate([ty, state_cols(lbr[nf, 0]), state_cols(lbi[nf, 0]),
                          state_cols(lbr[nbk, 1]), state_cols(lbi[nbk, 1])], axis=-1)

    def carry_rows(n, d):
        a = C_re[d][None] * pr[n, d][:, :, None, :] - C_im[d][None] * pi[n, d][:, :, None, :]
        bco = C_re[d][None] * pi[n, d][:, :, None, :] + C_im[d][None] * pr[n, d][:, :, None, :]
        to_rows = lambda x: x.transpose(1, 3, 0, 2).reshape(G, SSM_STATE, tc * SSM_CH)
        return to_rows(a), to_rows(-bco)

    fa, fb = carry_rows(jnp.arange(tc) + 1, 0)
    ba, bb = carry_rows(tc - jnp.arange(tc), 1)
    w2 = jnp.concatenate([fa, fb, ba, bb], axis=1)

    nsteps = max(1, int(math.ceil(math.log2(nc))))
    sr, si = lam_pow(tc * (2 ** jnp.arange(nsteps)))
    rows = jnp.stack([jnp.concatenate([sr, sr], -1), jnp.concatenate([-si, si], -1)], axis=2)
    coef = rows.transpose(3, 0, 1, 2, 4).reshape(G, nsteps * 4, LANES)
    return w1.astype(BF16), w2.astype(BF16), coef, nsteps


def _s5_kernel(u_ref, w1_ref, w2_ref, c_ref, o_ref, *, nc, nsteps):
    ny = S5_CHUNK * SSM_CH
    r = jnp.dot(u_ref[0], w1_ref[0], preferred_element_type=F32)
    y = r[:, :ny]
    row = lax.broadcasted_iota(jnp.int32, (nc, 1), 0)

    def shifted(x, d, fwd):
        if fwd:
            return jnp.where(row >= d, pltpu.roll(x, d, axis=0), 0.0)
        return jnp.where(row < nc - d, pltpu.roll(x, nc - d, axis=0), 0.0)

    def carry_in(x, dirn):
        s = shifted(x, 1, dirn == 0)
        for st in range(nsteps):
            d = 2 ** st
            if d >= nc:
                break
            sh = shifted(s, d, dirn == 0)
            a = c_ref[0, 4 * st + 2 * dirn:4 * st + 2 * dirn + 1, :]
            b = c_ref[0, 4 * st + 2 * dirn + 1:4 * st + 2 * dirn + 2, :]
            s = s + a * sh + b * pltpu.roll(sh, SSM_STATE, axis=1)
        return s

    sf = carry_in(r[:, ny:ny + LANES], 0)
    sb = carry_in(r[:, ny + LANES:ny + 2 * LANES], 1)
    s = jnp.concatenate([sf, sb], axis=1).astype(BF16)
    o_ref[0] = y + jnp.dot(s, w2_ref[0], preferred_element_type=F32)


def _s5(u, prep, nb, L):
    w1, w2, coef, nsteps = prep
    T, W = u.shape
    G = W // SSM_CH
    tc = S5_CHUNK
    nc = L // tc
    ny = tc * SSM_CH
    ug = u.reshape(nb * nc, tc, G, SSM_CH).transpose(2, 0, 1, 3).reshape(G, nb * nc, ny)
    yg = pl.pallas_call(
        functools.partial(_s5_kernel, nc=nc, nsteps=nsteps),
        grid=(nb, G),
        in_specs=[pl.BlockSpec((1, nc, ny), lambda b, g: (g, b, 0)),
                  pl.BlockSpec((1, ny, 2 * ny), lambda b, g: (g, 0, 0)),
                  pl.BlockSpec((1, ny, ny), lambda b, g: (g, 0, 0)),
                  pl.BlockSpec((1, 4 * nsteps, LANES), lambda b, g: (g, 0, 0))],
        out_specs=pl.BlockSpec((1, nc, ny), lambda b, g: (g, b, 0)),
        out_shape=jax.ShapeDtypeStruct((G, nb * nc, ny), F32),
        compiler_params=_cparams(("parallel", "parallel")),
        name="s5_chunked",
    )(ug, w1, w2, coef)
    return yg.reshape(G, nb * nc, tc, SSM_CH).transpose(1, 2, 0, 3).reshape(T, W)


def _glu_kernel(y_ref, w_ref, b_ref, g_ref, o_ref):
    y = y_ref[...]
    c0 = math.sqrt(2.0 / math.pi)
    gl = 0.5 * y * (1.0 + jnp.tanh(c0 * (y + 0.044715 * (y * y * y))))
    z = jnp.dot(gl.astype(BF16), w_ref[...], preferred_element_type=F32) + b_ref[...]
    s = gl * (1.0 / (1.0 + jnp.exp(-z)))
    ms = jnp.mean(s * s, axis=-1, keepdims=True)
    o_ref[...] = (s * lax.rsqrt(ms + RMS_EPS) * g_ref[...]).astype(o_ref.dtype)


def _glu_norm(y, w, b, g, tm=512):
    T, W = y.shape
    tm = min(tm, T)
    return pl.pallas_call(
        _glu_kernel,
        grid=(T // tm,),
        in_specs=[pl.BlockSpec((tm, W), lambda i: (i, 0)),
                  pl.BlockSpec((W, W), lambda i: (0, 0)),
                  pl.BlockSpec((1, W), lambda i: (0, 0)),
                  pl.BlockSpec((1, W), lambda i: (0, 0))],
        out_specs=pl.BlockSpec((tm, W), lambda i: (i, 0)),
        out_shape=jax.ShapeDtypeStruct((T, W), BF16),
        compiler_params=_cparams(("parallel",)),
        name="glu_norm",
    )(y, w, b.reshape(1, W).astype(F32), g.reshape(1, W).astype(F32))


def _out_proj_kernel(a_ref, s_ref, wa_ref, ws_ref, x_ref, o_ref):
    acc = jnp.dot(a_ref[...], wa_ref[...], preferred_element_type=F32)
    acc = acc + jnp.dot(s_ref[...], ws_ref[...], preferred_element_type=F32)
    o_ref[...] = x_ref[...] + acc


def _out_proj(att, ssm, w, x, tm=1024, tn=512):
    T, Wa = att.shape
    D = x.shape[1]
    tm = min(tm, T)
    tn = min(tn, D)
    rb = Wa // Wa
    return pl.pallas_call(
        _out_proj_kernel,
        grid=(T // tm, D // tn),
        in_specs=[pl.BlockSpec((tm, Wa), lambda i, j: (i, 0)),
                  pl.BlockSpec((tm, Wa), lambda i, j: (i, 0)),
                  pl.BlockSpec((Wa, tn), lambda i, j: (0, j)),
                  pl.BlockSpec((Wa, tn), lambda i, j: (rb, j)),
                  pl.BlockSpec((tm, tn), lambda i, j: (i, j))],
        out_specs=pl.BlockSpec((tm, tn), lambda i, j: (i, j)),
        out_shape=jax.ShapeDtypeStruct((T, D), F32),
        compiler_params=_cparams(("parallel", "arbitrary")),
        name="out_proj",
    )(att, ssm, w, w, x)


def _router_kernel(h_ref, g_ref, w_ref, b_ref, xn_ref, lg_ref):
    x = h_ref[...]
    ms = jnp.mean(x * x, axis=-1, keepdims=True)
    xn = x * lax.rsqrt(ms + RMS_EPS) * g_ref[...]
    xn_ref[...] = _pack_halves(xn)
    lg_ref[...] = jnp.dot(xn, w_ref[...], preferred_element_type=F32,
                          precision=lax.Precision.HIGHEST) + b_ref[...]


def _norm_router(h, g, rw, rb, tm=256):
    T, D = h.shape
    tm = min(tm, T)
    return pl.pallas_call(
        _router_kernel,
        grid=(T // tm,),
        in_specs=[pl.BlockSpec((tm, D), lambda i: (i, 0)),
                  pl.BlockSpec((1, D), lambda i: (0, 0)),
                  pl.BlockSpec((D, ROUTER_PAD), lambda i: (0, 0)),
                  pl.BlockSpec((1, ROUTER_PAD), lambda i: (0, 0))],
        out_specs=[pl.BlockSpec((tm, D // 2), lambda i: (i, 0)),
                   pl.BlockSpec((tm, ROUTER_PAD), lambda i: (i, 0))],
        out_shape=[jax.ShapeDtypeStruct((T, D // 2), jnp.uint32), jax.ShapeDtypeStruct((T, ROUTER_PAD), F32)],
        compiler_params=_cparams(("parallel",)),
        name="norm_router",
    )(h, g.reshape(1, D).astype(F32), rw, rb)


def _pack_halves(x):
    half = x.shape[1] // 2
    lo = pltpu.bitcast(x[:, :half].astype(BF16).astype(F32), jnp.uint32)
    hi = pltpu.bitcast(x[:, half:].astype(BF16).astype(F32), jnp.uint32)
    return hi | (lo >> 16)


def _unpack_halves(w):
    lo = pltpu.bitcast(w << 16, F32)
    hi = pltpu.bitcast(w & jnp.uint32(0xFFFF0000), F32)
    return lo, hi


def _row_copy(src_hbm, src_row, dst, dst_row, sem):
    return pltpu.make_async_copy(src_hbm.at[pl.ds(src_row, 1), :], dst.at[pl.ds(dst_row, 1), :], sem)


def _gather_start(idx_ref, src_hbm, dst, sem, n):
    for r in range(n):
        _row_copy(src_hbm, idx_ref[0, 0, r], dst, r, sem).start()


def _gather_wait(src_hbm, dst, sem, n):
    for r in range(n):
        _row_copy(src_hbm, 0, dst, r, sem).wait()


def _expert_up_kernel(be_ref, nu_ref, tok_ref, tokn_ref, x_hbm, wg_ref, wu_ref, h_ref, xbuf, sem, *, blk):
    b = pl.program_id(0)
    nused = nu_ref[0]
    slot = b % 2

    @pl.when(b == 0)
    def _():
        _gather_start(tok_ref, x_hbm, xbuf.at[0], sem.at[0], blk)

    @pl.when(b < nused)
    def _():
        _gather_wait(x_hbm, xbuf.at[slot], sem.at[slot], blk)
        _gather_start(tokn_ref, x_hbm, xbuf.at[1 - slot], sem.at[1 - slot], blk)
        lo, hi = _unpack_halves(xbuf[slot])
        x = jnp.concatenate([lo, hi], axis=1).astype(BF16)
        g = jnp.dot(x, wg_ref[0], preferred_element_type=F32)
        u = jnp.dot(x, wu_ref[0], preferred_element_type=F32)
        h_ref[...] = (g * (1.0 / (1.0 + jnp.exp(-g))) * u).astype(h_ref.dtype)

    @pl.when(b >= nused)
    def _():
        h_ref[...] = jnp.zeros_like(h_ref)

    @pl.when(b == nused)
    def _():
        _gather_wait(x_hbm, xbuf.at[slot], sem.at[slot], blk)


def _expert_down_kernel(be_ref, nu_ref, h_ref, sw_ref, wd_ref, o_ref):
    live = pl.program_id(0) < nu_ref[0]

    @pl.when(live)
    def _():
        o_ref[...] = _pack_halves(jnp.dot(h_ref[...], wd_ref[0], preferred_element_type=F32) * sw_ref[0])

    @pl.when(jnp.logical_not(live))
    def _():
        o_ref[...] = jnp.zeros_like(o_ref)


def _experts(xn2, blk_e, nused, slot_tok, slot_w, wg, wu, wd, blk):
    T, Dh = xn2.shape
    D = 2 * Dh
    nblk = slot_tok.shape[0]
    dff = wg.shape[2]

    def wmap(b, be, nu):
        return (be[jnp.minimum(b, nu[0] - 1)], 0, 0)

    hid = pl.pallas_call(
        functools.partial(_expert_up_kernel, blk=blk),
        grid_spec=pltpu.PrefetchScalarGridSpec(
            num_scalar_prefetch=2,
            grid=(nblk,),
            in_specs=[pl.BlockSpec((1, 1, blk), lambda b, be, nu: (b, 0, 0), memory_space=pltpu.SMEM),
                      pl.BlockSpec((1, 1, blk), lambda b, be, nu: (jnp.minimum(b + 1, nu[0] - 1), 0, 0),
                                   memory_space=pltpu.SMEM),
                      pl.BlockSpec(memory_space=pl.ANY),
                      pl.BlockSpec((1, D, dff), wmap),
                      pl.BlockSpec((1, D, dff), wmap)],
            out_specs=pl.BlockSpec((blk, dff), lambda b, be, nu: (b, 0)),
            scratch_shapes=[pltpu.VMEM((2, blk, Dh), jnp.uint32), pltpu.SemaphoreType.DMA((2,))],
        ),
        out_shape=jax.ShapeDtypeStruct((nblk * blk, dff), BF16),
        compiler_params=_cparams(("arbitrary",)),
        name="experts_up",
    )(blk_e, nused, slot_tok, slot_tok, xn2, wg, wu)
    return pl.pallas_call(
        _expert_down_kernel,
        grid_spec=pltpu.PrefetchScalarGridSpec(
            num_scalar_prefetch=2,
            grid=(nblk,),
            in_specs=[pl.BlockSpec((blk, dff), lambda b, be, nu: (b, 0)),
                      pl.BlockSpec((1, blk, 1), lambda b, be, nu: (b, 0, 0)),
                      pl.BlockSpec((1, dff, D), wmap)],
            out_specs=pl.BlockSpec((blk, Dh), lambda b, be, nu: (b, 0)),
        ),
        out_shape=jax.ShapeDtypeStruct((nblk * blk, Dh), jnp.uint32),
        compiler_params=_cparams(("arbitrary",)),
        name="experts_down",
    )(blk_e, nused, hid, slot_w, wd)


def _combine_kernel(d_ref, dn_ref, h_ref, ys_hbm, o_ref, buf, sem, *, tm, nt):
    i = pl.program_id(0)
    slot = i % 2
    half = h_ref.shape[1] // 2

    @pl.when(i == 0)
    def _():
        _gather_start(d_ref, ys_hbm, buf.at[0], sem.at[0], 2 * tm)

    _gather_wait(ys_hbm, buf.at[slot], sem.at[slot], 2 * tm)
    _gather_start(dn_ref, ys_hbm, buf.at[1 - slot], sem.at[1 - slot], 2 * tm)
    lo0, hi0 = _unpack_halves(buf[slot, :tm, :])
    lo1, hi1 = _unpack_halves(buf[slot, tm:, :])
    o_ref[:, :half] = h_ref[:, :half] + lo0 + lo1
    o_ref[:, half:] = h_ref[:, half:] + hi0 + hi1

    @pl.when(i == nt - 1)
    def _():
        _gather_wait(ys_hbm, buf.at[1 - slot], sem.at[1 - slot], 2 * tm)


def _combine(h, ys, dest, tm=128):
    T, D = h.shape
    tm = min(tm, T)
    nt = T // tm
    d3 = dest.reshape(nt, tm, 2).transpose(0, 2, 1).reshape(nt, 1, 2 * tm)
    return pl.pallas_call(
        functools.partial(_combine_kernel, tm=tm, nt=nt),
        grid=(nt,),
        in_specs=[pl.BlockSpec((1, 1, 2 * tm), lambda i: (i, 0, 0), memory_space=pltpu.SMEM),
                  pl.BlockSpec((1, 1, 2 * tm), lambda i: (jnp.minimum(i + 1, nt - 1), 0, 0),
                               memory_space=pltpu.SMEM),
                  pl.BlockSpec((tm, D), lambda i: (i, 0)),
                  pl.BlockSpec(memory_space=pl.ANY)],
        out_specs=pl.BlockSpec((tm, D), lambda i: (i, 0)),
        out_shape=jax.ShapeDtypeStruct((T, D), F32),
        scratch_shapes=[pltpu.VMEM((2, 2 * tm, D // 2), jnp.uint32), pltpu.SemaphoreType.DMA((2,))],
        compiler_params=_cparams(("arbitrary",)),
        name="moe_combine",
    )(d3, d3, h, ys)


def _route(logits, blk):
    T = logits.shape[0]
    ng, epg, ne = N_EXPERT_GROUPS, EXPERTS_PER_GROUP, N_EXPERTS
    gp = jax.nn.softmax(logits[:, :ng], axis=-1)
    gsel = jnp.argmax(gp, axis=-1).astype(jnp.int32)
    gprob = jnp.take_along_axis(gp, gsel[:, None], axis=1)[:, 0]
    elog = logits[:, ng:ng + ne].reshape(T, ng, epg)
    elog = jnp.take_along_axis(elog, gsel[:, None, None], axis=1)[:, 0]
    topv, topi = lax.top_k(jax.nn.softmax(elog, axis=-1), TOP_K_INNER)
    gate = topv / jnp.sum(topv, axis=-1, keepdims=True) * gprob[:, None]
    eid = (gsel[:, None] * epg + topi.astype(jnp.int32)).reshape(-1)
    N = T * TOP_K_INNER
    onehot = (eid[:, None] == jnp.arange(ne, dtype=jnp.int32)[None, :]).astype(jnp.int32)
    csum = jnp.cumsum(onehot, axis=0)
    rank = jnp.take_along_axis(csum, eid[:, None], axis=1)[:, 0] - 1
    counts = csum[-1]
    pcounts = ((counts + blk - 1) // blk) * blk
    pends = jnp.cumsum(pcounts)
    pstarts = pends - pcounts
    dest = pstarts[eid] + rank
    nblk = N // blk + ne
    tok = jnp.arange(N, dtype=jnp.int32) // TOP_K_INNER
    slot_tok = jnp.zeros((nblk * blk,), jnp.int32).at[dest].set(tok)
    slot_w = jnp.zeros((nblk * blk,), F32).at[dest].set(gate.reshape(-1))
    blk_e = jnp.clip(jnp.searchsorted(pends, jnp.arange(nblk, dtype=jnp.int32) * blk, side='right'),
                     0, ne - 1).astype(jnp.int32)
    nused = (pends[-1] // blk).astype(jnp.int32).reshape(1)
    return (blk_e, nused, slot_tok.reshape(nblk, 1, blk), slot_w.reshape(nblk, blk, 1),
            dest.reshape(T, TOP_K_INNER).astype(jnp.int32))


def _moe(h, norm2_g, rw, rb, wg, wu, wd, blk=256):
    xn2, logits = _norm_router(h, norm2_g, rw, rb)
    blk = min(blk, h.shape[0])
    blk_e, nused, slot_tok, slot_w, dest = _route(logits, blk)
    ys = _experts(xn2, blk_e, nused, slot_tok, slot_w, wg, wu, wd, blk)
    return _combine(h, ys, dest)


def _layer(x3, lam_init, p):
    nb, L, D = x3.shape
    T = nb * L
    x = x3.reshape(T, D)
    w_in = p['w_in']
    att_w = N_HEADS * DV
    xn = _rmsnorm_cast(x, p['norm1_g'])
    qkn = _proj_qk(xn, w_in, p['q_norm_g'], p['k_norm_g'], 2 * att_w)
    vu = _matmul(xn, w_in, 2 * att_w, w_in.shape[1] - 2 * att_w, BF16, "in_proj_vu")
    att = _attention(qkn, vu, p['lam'], p['attn_sub_g'], p['q_norm_g'], p['k_norm_g'], lam_init, nb, L)
    y = _s5(vu[:, att_w:], p['s5'](L // S5_CHUNK), nb, L)
    ssm = _glu_norm(y, p['glu_w'], p['glu_b'], p['ssm_out_g'])
    h = _out_proj(att, ssm, p['w_out'], x)
    out = _moe(h, p['norm2_g'], p['rw'], p['rb'], p['wg'], p['wu'], p['wd'])
    return out.reshape(nb, L, D)


def kernel(x_prompt, x_sample, norm1_g, w_in, q_norm_g, k_norm_g, lam_q1, lam_k1, lam_q2, lam_k2, attn_sub_g, ssm_A_re, ssm_A_im, ssm_log_dt, ssm_B_re, ssm_B_im, ssm_C_re, ssm_C_im, ssm_D, glu_w, glu_b, ssm_out_g, w_out, norm2_g, router_group_w, router_group_b, router_expert_w, router_expert_b, exp_w_gate, exp_w_up, exp_w_down):
    depth = w_in.shape[0]

    def layer_params(l):
        lam_init = 0.8 - 0.6 * math.exp(-0.3 * l)
        lam = (jnp.exp(jnp.sum(lam_q1[l].astype(F32) * lam_k1[l].astype(F32)))
               - jnp.exp(jnp.sum(lam_q2[l].astype(F32) * lam_k2[l].astype(F32))) + lam_init)
        D = w_in.shape[1]
        ng, ne = router_group_w.shape[2], router_expert_w.shape[2]
        rw = jnp.zeros((D, ROUTER_PAD), F32)
        rw = rw.at[:, :ng].set(router_group_w[l].astype(F32)).at[:, ng:ng + ne].set(router_expert_w[l].astype(F32))
        rb = jnp.zeros((1, ROUTER_PAD), F32)
        rb = rb.at[0, :ng].set(router_group_b[l].astype(F32)).at[0, ng:ng + ne].set(router_expert_b[l].astype(F32))
        s5_cache = {}

        def s5(nc):
            if nc not in s5_cache:
                s5_cache[nc] = _s5_prep(ssm_A_re[l], ssm_A_im[l], ssm_log_dt[l], ssm_B_re[l], ssm_B_im[l],
                                        ssm_C_re[l], ssm_C_im[l], ssm_D[l], nc)
            return s5_cache[nc]

        return lam_init, dict(
            norm1_g=norm1_g[l], w_in=w_in[l].astype(BF16), q_norm_g=q_norm_g[l], k_norm_g=k_norm_g[l], lam=lam,
            attn_sub_g=attn_sub_g[l], s5=s5, glu_w=glu_w[l].astype(BF16), glu_b=glu_b[l], ssm_out_g=ssm_out_g[l],
            w_out=w_out[l].astype(BF16), norm2_g=norm2_g[l], rw=rw, rb=rb,
            wg=exp_w_gate[l].astype(BF16), wu=exp_w_up[l].astype(BF16), wd=exp_w_down[l].astype(BF16))

    params = [layer_params(l) for l in range(depth)]

    def run(x):
        for lam_init, p in params:
            x = _layer(x, lam_init, p)
        return x

    return (run(x_prompt), run(x_sample))
```

```python
import functools
import math

import jax
import jax.numpy as jnp
from jax import lax
from jax.experimental import pallas as pl
from jax.experimental.pallas import tpu as pltpu

F32 = jnp.float32
BF16 = jnp.bfloat16

RMS_EPS = 1e-6
LOG2E = 1.4426950408889634
LANES = 128
VMEM_LIMIT = 56 * 1024 * 1024

N_HEADS = 16
DQK = 64
DV = 2 * DQK
SSM_CH = 16
SSM_STATE = 64
S5_CHUNK = 16
N_EXPERT_GROUPS = 4
EXPERTS_PER_GROUP = 8
N_EXPERTS = N_EXPERT_GROUPS * EXPERTS_PER_GROUP
TOP_K_INNER = 2
ROUTER_PAD = LANES


def _cparams(sem):
    return pltpu.CompilerParams(dimension_semantics=sem, vmem_limit_bytes=VMEM_LIMIT)


def _rmsnorm_kernel(x_ref, g_ref, o_ref):
    x = x_ref[...]
    ms = jnp.mean(x * x, axis=-1, keepdims=True)
    o_ref[...] = (x * lax.rsqrt(ms + RMS_EPS) * g_ref[...]).astype(o_ref.dtype)


def _rmsnorm_cast(x, g, tm=512):
    T, D = x.shape
    tm = min(tm, T)
    return pl.pallas_call(
        _rmsnorm_kernel,
        grid=(T // tm,),
        in_specs=[pl.BlockSpec((tm, D), lambda i: (i, 0)),
                  pl.BlockSpec((1, D), lambda i: (0, 0))],
        out_specs=pl.BlockSpec((tm, D), lambda i: (i, 0)),
        out_shape=jax.ShapeDtypeStruct((T, D), BF16),
        compiler_params=_cparams(("parallel",)),
        name="rmsnorm1",
    )(x, g.reshape(1, D).astype(F32))


def _mm_kernel(a_ref, b_ref, o_ref):
    o_ref[...] = jnp.dot(a_ref[...], b_ref[...], preferred_element_type=F32).astype(o_ref.dtype)


def _matmul(a, b, col0, ncols, out_dtype, name, tm=1024, tn=512):
    M, K = a.shape
    tm = min(tm, M)
    tn = min(tn, ncols)
    cb = col0 // tn
    return pl.pallas_call(
        _mm_kernel,
        grid=(M // tm, ncols // tn),
        in_specs=[pl.BlockSpec((tm, K), lambda i, j: (i, 0)),
                  pl.BlockSpec((K, tn), lambda i, j: (0, j + cb))],
        out_specs=pl.BlockSpec((tm, tn), lambda i, j: (i, j)),
        out_shape=jax.ShapeDtypeStruct((M, ncols), out_dtype),
        compiler_params=_cparams(("parallel", "arbitrary")),
        name=name,
    )(a, b)


def _mm_qknorm_kernel(a_ref, b_ref, g_ref, o_ref, *, tn):
    acc = jnp.dot(a_ref[...], b_ref[...], preferred_element_type=F32)
    lane = lax.broadcasted_iota(jnp.int32, (1, LANES), 1)
    left = lane < DQK
    for c in range(tn // LANES):
        x = acc[:, c * LANES:(c + 1) * LANES]
        x2 = x * x
        s_all = jnp.sum(x2, axis=-1, keepdims=True)
        s_left = jnp.sum(jnp.where(left, x2, 0.0), axis=-1, keepdims=True)
        ms = jnp.where(left, s_left, s_all - s_left) * (1.0 / DQK)
        o_ref[:, c * LANES:(c + 1) * LANES] = (
            x * lax.rsqrt(ms + RMS_EPS) * g_ref[:, c * LANES:(c + 1) * LANES]).astype(o_ref.dtype)


def _proj_qk(xn, w, gq, gk, ncols, tm=1024, tn=512):
    M, K = xn.shape
    tm = min(tm, M)
    reps = ncols // (2 * DQK)
    g = jnp.concatenate([jnp.tile(gq.astype(F32) * (DQK ** -0.5 * LOG2E), reps),
                         jnp.tile(gk.astype(F32), reps)]).reshape(1, ncols)
    return pl.pallas_call(
        functools.partial(_mm_qknorm_kernel, tn=tn),
        grid=(M // tm, ncols // tn),
        in_specs=[pl.BlockSpec((tm, K), lambda i, j: (i, 0)),
                  pl.BlockSpec((K, tn), lambda i, j: (0, j)),
                  pl.BlockSpec((1, tn), lambda i, j: (0, j))],
        out_specs=pl.BlockSpec((tm, tn), lambda i, j: (i, j)),
        out_shape=jax.ShapeDtypeStruct((M, ncols), BF16),
        compiler_params=_cparams(("parallel", "arbitrary")),
        name="in_proj_qk",
    )(xn, w, g)


POS_SPLIT = 16


def _attn_kernel(lam_ref, sl_ref, dist_ref, q_ref, k_ref, v_ref, g_ref, o_ref, acc_ref, m_ref, s_ref, rm_ref,
                 *, tq, tk, nk, out_scale):
    h = pl.program_id(1)
    qi = pl.program_id(2)
    q0 = qi * tq
    s0, s1, s2, slope = sl_ref[h, 0], sl_ref[h, 1], sl_ref[h, 2], sl_ref[h, 3]
    lam = lam_ref[0]
    q = q_ref[...]
    lane = lax.broadcasted_iota(jnp.int32, (1, LANES), 1)
    zero = jnp.zeros_like(q)
    qs = jnp.concatenate([jnp.where(lane < DQK, q, zero), jnp.where(lane >= DQK, q, zero)], axis=0)

    def pos_parts(rows):
        r = lax.broadcasted_iota(jnp.int32, (rows, 1), 0)
        return (r & ~(POS_SPLIT - 1)).astype(F32), (r & (POS_SPLIT - 1)).astype(F32)

    def by_lane(vals, rows):
        out = jnp.zeros((rows, LANES), F32)
        for j, val in enumerate(vals):
            out = jnp.where(lane == j, val, out)
        return out

    qa, qb = pos_parts(tq)
    ka, kb = pos_parts(tk)
    qaug = by_lane([s0, s1, s2, s0, s1, s2, qa, qa, qa, qb, qb, qb], tq)
    kaug = by_lane([ka, ka, ka, kb, kb, kb, -s0, -s1, -s2, -s0, -s1, -s2], tk).astype(BF16)
    qcat = jnp.concatenate([qs, jnp.concatenate([qaug, qaug], axis=0).astype(BF16)], axis=1)
    ones_col = jnp.broadcast_to(jnp.where(lane == 0, 1.0, 0.0).astype(BF16), (tk, LANES))
    nt = (((1,), (1,)), ((), ()))

    m_ref[...] = jnp.full(m_ref.shape, -1e30, F32)
    acc_ref[...] = jnp.zeros(acc_ref.shape, F32)

    def row_max(sp):
        return jnp.broadcast_to(jnp.max(sp, axis=-1, keepdims=True), (2 * tq, LANES))

    def update(sp, rm, c, start):
        m_old = m_ref[...]
        m_new = jnp.maximum(m_old, rm + c)
        alpha = jnp.exp2(m_old - m_new)
        mu = m_new - c
        p = jnp.exp2(sp - jnp.concatenate([mu] * (tk // LANES), axis=1)).astype(BF16)
        vcat = jnp.concatenate([v_ref[pl.ds(start, tk), :], ones_col], axis=1)
        acc_ref[...] = (jnp.concatenate([alpha, alpha], axis=1) * acc_ref[...]
                        + jnp.dot(p, vcat, preferred_element_type=F32))
        m_ref[...] = m_new

    dist = dist_ref[h]
    lo = jnp.maximum(q0 - dist, 0) // tk
    hi = jnp.minimum((q0 + tq - 1 + dist) // tk + 1, nk)
    n_left = qi - lo
    n_side = n_left + (hi - qi - 1)

    def tile_of(t):
        t = jnp.minimum(t, n_side - 1)
        return jnp.clip(jnp.where(t < n_left, lo + t, qi + 1 + t - n_left), 0, nk - 1), t < n_left

    def scores(slot, t):
        ki, is_left = tile_of(t)
        start = pl.multiple_of(ki * tk, tk)
        kc = jnp.concatenate([k_ref[pl.ds(start, tk), :], jnp.where(is_left, kaug, -kaug)], axis=1)
        sp = lax.dot_general(qcat, kc, nt, preferred_element_type=F32)
        s_ref[slot] = sp
        rm_ref[slot] = row_max(sp)

    def consume(slot, t):
        ki, is_left = tile_of(t)
        dist0 = jnp.where(is_left, q0 - ki * tk, ki * tk - q0)
        update(s_ref[slot], rm_ref[slot], -slope * lax.convert_element_type(dist0, F32),
               pl.multiple_of(ki * tk, tk))

    def pair(i, carry):
        scores(1, 2 * i + 1)
        consume(0, 2 * i)
        scores(0, 2 * i + 2)
        consume(1, 2 * i + 1)
        return carry

    scores(0, 0)
    start_d = pl.multiple_of(q0, tk)
    rel = (lax.broadcasted_iota(jnp.int32, (tq, 1), 0) - lax.broadcasted_iota(jnp.int32, (1, tk), 1)).astype(F32)
    bias = -slope * jnp.abs(rel)
    sp_d = (lax.dot_general(qs, k_ref[pl.ds(start_d, tk), :], nt, preferred_element_type=F32)
            + jnp.concatenate([bias, bias], axis=0))
    update(sp_d, row_max(sp_d), 0.0, start_d)
    lax.fori_loop(0, n_side // 2, pair, 0)

    @pl.when(n_side % 2 == 1)
    def _():
        consume(0, n_side - 1)

    acc = acc_ref[...]
    o = acc[:tq, :DV] / acc[:tq, DV:DV + 1] - lam * (acc[tq:, :DV] / acc[tq:, DV:DV + 1])
    ms = jnp.mean(o * o, axis=-1, keepdims=True)
    o_ref[...] = (o * lax.rsqrt(ms + RMS_EPS) * (g_ref[...] * out_scale)).astype(o_ref.dtype)


def _split3_bf16(x):
    p0 = x.astype(BF16).astype(F32)
    p1 = (x - p0).astype(BF16).astype(F32)
    p2 = (x - p0 - p1).astype(BF16).astype(F32)
    return p0, p1, p2


F32_EXP_ZERO = 104.0


def _zero_weight_distance(gq, gk, slopes_nat, L):
    bound = 8.0 * jnp.max(jnp.abs(gq.astype(F32))) * jnp.max(jnp.abs(gk.astype(F32))) * 1.02
    d = jnp.ceil((F32_EXP_ZERO + 2.0 * bound) / slopes_nat) + 1.0
    return jnp.clip(d, 1.0, float(L)).astype(jnp.int32)


def _attention(qkn, vu, lam, sub_g, gq, gk, lam_init, nb, L, tile=512):
    T = qkn.shape[0]
    tq = tk = min(tile, L)
    nq = L // tq
    slopes_nat = jnp.exp2(-8.0 * jnp.arange(1, N_HEADS + 1, dtype=F32) / N_HEADS)
    p0, p1, p2 = _split3_bf16(slopes_nat * LOG2E)
    sl = jnp.stack([p0, p1, p2, p0 + p1 + p2], axis=1)
    dist = _zero_weight_distance(gq, gk, slopes_nat, L)
    kern = functools.partial(_attn_kernel, tq=tq, tk=tk, nk=L // tk, out_scale=1.0 - lam_init)
    return pl.pallas_call(
        kern,
        grid=(nb, N_HEADS, nq),
        in_specs=[pl.BlockSpec(memory_space=pltpu.SMEM),
                  pl.BlockSpec(memory_space=pltpu.SMEM),
                  pl.BlockSpec(memory_space=pltpu.SMEM),
                  pl.BlockSpec((tq, LANES), lambda b, h, i: (b * nq + i, h)),
                  pl.BlockSpec((L, LANES), lambda b, h, i: (b, N_HEADS + h)),
                  pl.BlockSpec((L, LANES), lambda b, h, i: (b, h)),
                  pl.BlockSpec((1, DV), lambda b, h, i: (0, 0))],
        out_specs=pl.BlockSpec((tq, LANES), lambda b, h, i: (b * nq + i, h)),
        out_shape=jax.ShapeDtypeStruct((T, N_HEADS * DV), BF16),
        scratch_shapes=[pltpu.VMEM((2 * tq, 2 * LANES), F32), pltpu.VMEM((2 * tq, LANES), F32),
                        pltpu.VMEM((2, 2 * tq, tk), F32), pltpu.VMEM((2, 2 * tq, LANES), F32)],
        compiler_params=_cparams(("parallel", "parallel", "arbitrary")),
        name="diff_attn",
    )(lam.reshape(1).astype(F32), sl, dist, qkn, qkn, vu, sub_g.reshape(1, DV).astype(F32))


def _s5_prep(A_re, A_im, log_dt, B_re, B_im, C_re, C_im, d_skip, nc):
    tc = S5_CHUNK
    hp = lax.Precision.HIGHEST
    A_re, A_im, log_dt = A_re.astype(F32), A_im.astype(F32), log_dt.astype(F32)
    B_re, B_im, C_re, C_im = B_re.astype(F32), B_im.astype(F32), C_re.astype(F32), C_im.astype(F32)
    G = A_re.shape[1]
    dt = jnp.exp(log_dt)[..., None]
    ang = A_im * dt
    mag = jnp.exp(A_re * dt)
    lr, li = mag * jnp.cos(ang), mag * jnp.sin(ang)
    den = A_re * A_re + A_im * A_im
    cr = ((lr - 1.0) * A_re + li * A_im) / den
    ci = (li * A_re - (lr - 1.0) * A_im) / den
    bbr = cr[..., None] * B_re - ci[..., None] * B_im
    bbi = cr[..., None] * B_im + ci[..., None] * B_re

    def lam_pow(n):
        nn = n.astype(F32)[:, None, None, None]
        pm = jnp.exp(A_re * dt * nn)
        pa = ang * nn
        return pm * jnp.cos(pa), pm * jnp.sin(pa)

    pr, pi = lam_pow(jnp.arange(tc + 1))
    lbr = pr[..., None] * bbr - pi[..., None] * bbi
    lbi = pr[..., None] * bbi + pi[..., None] * bbr
    kk = (jnp.einsum('dghp,ndgpb->ndghb', C_re, lbr, precision=hp)
          - jnp.einsum('dghp,ndgpb->ndghb', C_im, lbi, precision=hp))
    j = jnp.arange(tc)[:, None]
    t = jnp.arange(tc)[None, :]
    kf = jnp.where((t >= j)[:, :, None, None, None], kk[jnp.clip(t - j, 0, tc), 0], 0.0)
    kb = jnp.where((j >= t)[:, :, None, None, None], kk[jnp.clip(j - t, 0, tc), 1], 0.0)
    eye_t = (j == t).astype(F32)[:, :, None, None, None]
    eye_c = jnp.eye(SSM_CH, dtype=F32)[None, None, None]
    dd = eye_t * eye_c * d_skip.astype(F32)[None, None, :, :, None]
    ty = (kf + kb + dd).transpose(2, 0, 4, 1, 3).reshape(G, tc * SSM_CH, tc * SSM_CH)

    def state_cols(x):
        return x.transpose(1, 0, 3, 2).reshape(G, tc * SSM_CH, SSM_STATE)

    nf = tc - 1 - jnp.arange(tc)
    nbk = jnp.arange(tc)
    w1 = jnp.concatenate([ty, state_cols(lbr[nf, 0]), state_cols(lbi[nf, 0]),
                          state_cols(lbr[nbk, 1]), state_cols(lbi[nbk, 1])], axis=-1)

    def carry_rows(n, d):
        a = C_re[d][None] * pr[n, d][:, :, None, :] - C_im[d][None] * pi[n, d][:, :, None, :]
        bco = C_re[d][None] * pi[n, d][:, :, None, :] + C_im[d][None] * pr[n, d][:, :, None, :]
        to_rows = lambda x: x.transpose(1, 3, 0, 2).reshape(G, SSM_STATE, tc * SSM_CH)
        return to_rows(a), to_rows(-bco)

    fa, fb = carry_rows(jnp.arange(tc) + 1, 0)
    ba, bb = carry_rows(tc - jnp.arange(tc), 1)
    w2 = jnp.concatenate([fa, fb, ba, bb], axis=1)

    nsteps = max(1, int(math.ceil(math.log2(nc))))
    sr, si = lam_pow(tc * (2 ** jnp.arange(nsteps)))
    rows = jnp.stack([jnp.concatenate([sr, sr], -1), jnp.concatenate([-si, si], -1)], axis=2)
    coef = rows.transpose(3, 0, 1, 2, 4).reshape(G, nsteps * 4, LANES)
    return w1.astype(BF16), w2.astype(BF16), coef, nsteps


GROUPS_PER_TILE = LANES // SSM_CH


def _granule_transpose(arrs):
    lane = lax.broadcasted_iota(jnp.int32, (1, LANES), 1)
    a = list(arrs)
    for dist, shift in ((4, 4 * SSM_CH), (2, 2 * SSM_CH), (1, SSM_CH)):
        keep_lo = (lane & (2 * shift - 1)) < shift
        new = list(a)
        for i in range(GROUPS_PER_TILE):
            if (i // dist) % 2 == 0:
                lo, hi = a[i], a[i + dist]
                new[i] = jnp.where(keep_lo, lo, pltpu.roll(hi, shift, axis=1))
                new[i + dist] = jnp.where(keep_lo, pltpu.roll(lo, LANES - shift, axis=1), hi)
        a = new
    return a


def _s5_kernel(u_ref, w1_ref, w2_ref, c_ref, o_ref, ug_ref, yg_ref, *, nc, nsteps):
    ny = S5_CHUNK * SSM_CH
    halves = S5_CHUNK // GROUPS_PER_TILE
    for hf in range(halves):
        xs = [u_ref[pl.ds(GROUPS_PER_TILE * hf + j, nc, stride=S5_CHUNK), :] for j in range(GROUPS_PER_TILE)]
        for g, x in enumerate(_granule_transpose(xs)):
            ug_ref[g, :, hf * LANES:(hf + 1) * LANES] = x.astype(BF16)

    row = lax.broadcasted_iota(jnp.int32, (nc, 1), 0)

    def shifted(x, d, fwd):
        if fwd:
            return jnp.where(row >= d, pltpu.roll(x, d, axis=0), 0.0)
        return jnp.where(row < nc - d, pltpu.roll(x, nc - d, axis=0), 0.0)

    def group(g, carry):
        r = jnp.dot(ug_ref[g], w1_ref[g], preferred_element_type=F32)

        def carry_in(x, dirn):
            s = shifted(x, 1, dirn == 0)
            for st in range(nsteps):
                d = 2 ** st
                if d >= nc:
                    break
                sh = shifted(s, d, dirn == 0)
                a = c_ref[g, 4 * st + 2 * dirn:4 * st + 2 * dirn + 1, :]
                b = c_ref[g, 4 * st + 2 * dirn + 1:4 * st + 2 * dirn + 2, :]
                s = s + a * sh + b * pltpu.roll(sh, SSM_STATE, axis=1)
            return s

        sf = carry_in(r[:, ny:ny + LANES], 0)
        sb = carry_in(r[:, ny + LANES:ny + 2 * LANES], 1)
        s = jnp.concatenate([sf, sb], axis=1).astype(BF16)
        yg_ref[g] = r[:, :ny] + jnp.dot(s, w2_ref[g], preferred_element_type=F32)
        return carry

    lax.fori_loop(0, GROUPS_PER_TILE, group, 0)

    for hf in range(halves):
        ys = [yg_ref[g, :, hf * LANES:(hf + 1) * LANES] for g in range(GROUPS_PER_TILE)]
        for j, x in enumerate(_granule_transpose(ys)):
            o_ref[pl.ds(GROUPS_PER_TILE * hf + j, nc, stride=S5_CHUNK), :] = x


def _s5(u, prep, nb, L):
    w1, w2, coef, nsteps = prep
    T, W = u.shape
    gt = GROUPS_PER_TILE
    nc = L // S5_CHUNK
    ny = S5_CHUNK * SSM_CH
    return pl.pallas_call(
        functools.partial(_s5_kernel, nc=nc, nsteps=nsteps),
        grid=(nb, W // LANES),
        in_specs=[pl.BlockSpec((L, LANES), lambda b, t: (b, t), pipeline_mode=pl.Buffered(1)),
                  pl.BlockSpec((gt, ny, 2 * ny), lambda b, t: (t, 0, 0)),
                  pl.BlockSpec((gt, ny, ny), lambda b, t: (t, 0, 0)),
                  pl.BlockSpec((gt, 4 * nsteps, LANES), lambda b, t: (t, 0, 0))],
        out_specs=pl.BlockSpec((L, LANES), lambda b, t: (b, t), pipeline_mode=pl.Buffered(1)),
        out_shape=jax.ShapeDtypeStruct((T, W), F32),
        scratch_shapes=[pltpu.VMEM((gt, nc, ny), BF16), pltpu.VMEM((gt, nc, ny), F32)],
        compiler_params=_cparams(("parallel", "parallel")),
        name="s5_chunked",
    )(u, w1, w2, coef)


def _glu_kernel(y_ref, w_ref, b_ref, g_ref, o_ref):
    y = y_ref[...]
    c0 = math.sqrt(2.0 / math.pi)
    gl = 0.5 * y * (1.0 + jnp.tanh(c0 * (y + 0.044715 * (y * y * y))))
    z = jnp.dot(gl.astype(BF16), w_ref[...], preferred_element_type=F32) + b_ref[...]
    s = gl * (1.0 / (1.0 + jnp.exp(-z)))
    ms = jnp.mean(s * s, axis=-1, keepdims=True)
    o_ref[...] = (s * lax.rsqrt(ms + RMS_EPS) * g_ref[...]).astype(o_ref.dtype)


def _glu_norm(y, w, b, g, tm=512):
    T, W = y.shape
    tm = min(tm, T)
    return pl.pallas_call(
        _glu_kernel,
        grid=(T // tm,),
        in_specs=[pl.BlockSpec((tm, W), lambda i: (i, 0)),
                  pl.BlockSpec((W, W), lambda i: (0, 0)),
                  pl.BlockSpec((1, W), lambda i: (0, 0)),
                  pl.BlockSpec((1, W), lambda i: (0, 0))],
        out_specs=pl.BlockSpec((tm, W), lambda i: (i, 0)),
        out_shape=jax.ShapeDtypeStruct((T, W), BF16),
        compiler_params=_cparams(("parallel",)),
        name="glu_norm",
    )(y, w, b.reshape(1, W).astype(F32), g.reshape(1, W).astype(F32))


def _out_proj_kernel(a_ref, s_ref, wa_ref, ws_ref, x_ref, o_ref):
    acc = jnp.dot(a_ref[...], wa_ref[...], preferred_element_type=F32)
    acc = acc + jnp.dot(s_ref[...], ws_ref[...], preferred_element_type=F32)
    o_ref[...] = x_ref[...] + acc


def _out_proj(att, ssm, w, x, tm=1024, tn=512):
    T, Wa = att.shape
    D = x.shape[1]
    tm = min(tm, T)
    tn = min(tn, D)
    rb = Wa // Wa
    return pl.pallas_call(
        _out_proj_kernel,
        grid=(T // tm, D // tn),
        in_specs=[pl.BlockSpec((tm, Wa), lambda i, j: (i, 0)),
                  pl.BlockSpec((tm, Wa), lambda i, j: (i, 0)),
                  pl.BlockSpec((Wa, tn), lambda i, j: (0, j)),
                  pl.BlockSpec((Wa, tn), lambda i, j: (rb, j)),
                  pl.BlockSpec((tm, tn), lambda i, j: (i, j))],
        out_specs=pl.BlockSpec((tm, tn), lambda i, j: (i, j)),
        out_shape=jax.ShapeDtypeStruct((T, D), F32),
        compiler_params=_cparams(("parallel", "arbitrary")),
        name="out_proj",
    )(att, ssm, w, w, x)


def _router_kernel(h_ref, g_ref, w_ref, b_ref, xn_ref, lg_ref):
    x = h_ref[...]
    ms = jnp.mean(x * x, axis=-1, keepdims=True)
    xn = x * lax.rsqrt(ms + RMS_EPS) * g_ref[...]
    xn_ref[...] = _pack_halves(xn)
    lg_ref[...] = jnp.dot(xn, w_ref[...], preferred_element_type=F32,
                          precision=lax.Precision.HIGHEST) + b_ref[...]


def _norm_router(h, g, rw, rb, tm=256):
    T, D = h.shape
    tm = min(tm, T)
    return pl.pallas_call(
        _router_kernel,
        grid=(T // tm,),
        in_specs=[pl.BlockSpec((tm, D), lambda i: (i, 0)),
                  pl.BlockSpec((1, D), lambda i: (0, 0)),
                  pl.BlockSpec((D, ROUTER_PAD), lambda i: (0, 0)),
                  pl.BlockSpec((1, ROUTER_PAD), lambda i: (0, 0))],
        out_specs=[pl.BlockSpec((tm, D // 2), lambda i: (i, 0)),
                   pl.BlockSpec((tm, ROUTER_PAD), lambda i: (i, 0))],
        out_shape=[jax.ShapeDtypeStruct((T, D // 2), jnp.uint32), jax.ShapeDtypeStruct((T, ROUTER_PAD), F32)],
        compiler_params=_cparams(("parallel",)),
        name="norm_router",
    )(h, g.reshape(1, D).astype(F32), rw, rb)


def _pack_halves(x):
    half = x.shape[1] // 2
    lo = pltpu.bitcast(x[:, :half].astype(BF16).astype(F32), jnp.uint32)
    hi = pltpu.bitcast(x[:, half:].astype(BF16).astype(F32), jnp.uint32)
    return hi | (lo >> 16)


def _unpack_halves(w):
    lo = pltpu.bitcast(w << 16, F32)
    hi = pltpu.bitcast(w & jnp.uint32(0xFFFF0000), F32)
    return lo, hi


def _row_copy(src_hbm, src_row, dst, dst_row, sem):
    return pltpu.make_async_copy(src_hbm.at[pl.ds(src_row, 1), :], dst.at[pl.ds(dst_row, 1), :], sem)


def _gather_start(idx_ref, src_hbm, dst, sem, n):
    for r in range(n):
        _row_copy(src_hbm, idx_ref[0, 0, r], dst, r, sem).start()


def _gather_wait(src_hbm, dst, sem, n):
    for r in range(n):
        _row_copy(src_hbm, 0, dst, r, sem).wait()


def _expert_up_kernel(be_ref, nu_ref, tok_ref, tokn_ref, x_hbm, wg_ref, wu_ref, h_ref, xbuf, sem, *, blk):
    b = pl.program_id(0)
    nused = nu_ref[0]
    slot = b % 2

    @pl.when(b == 0)
    def _():
        _gather_start(tok_ref, x_hbm, xbuf.at[0], sem.at[0], blk)

    @pl.when(b < nused)
    def _():
        _gather_wait(x_hbm, xbuf.at[slot], sem.at[slot], blk)
        _gather_start(tokn_ref, x_hbm, xbuf.at[1 - slot], sem.at[1 - slot], blk)
        lo, hi = _unpack_halves(xbuf[slot])
        x = jnp.concatenate([lo, hi], axis=1).astype(BF16)
        g = jnp.dot(x, wg_ref[0], preferred_element_type=F32)
        u = jnp.dot(x, wu_ref[0], preferred_element_type=F32)
        h_ref[...] = (g * (1.0 / (1.0 + jnp.exp(-g))) * u).astype(h_ref.dtype)

    @pl.when(b >= nused)
    def _():
        h_ref[...] = jnp.zeros_like(h_ref)

    @pl.when(b == nused)
    def _():
        _gather_wait(x_hbm, xbuf.at[slot], sem.at[slot], blk)


def _expert_down_kernel(be_ref, nu_ref, h_ref, wd_ref, o_ref):
    live = pl.program_id(0) < nu_ref[0]

    @pl.when(live)
    def _():
        o_ref[...] = _pack_halves(jnp.dot(h_ref[...], wd_ref[0], preferred_element_type=F32))

    @pl.when(jnp.logical_not(live))
    def _():
        o_ref[...] = jnp.zeros_like(o_ref)


def _experts(xn2, blk_e, nused, slot_tok, wg, wu, wd, blk):
    T, Dh = xn2.shape
    D = 2 * Dh
    nblk = slot_tok.shape[0]
    dff = wg.shape[2]

    def wmap(b, be, nu):
        return (be[jnp.minimum(b, nu[0] - 1)], 0, 0)

    hid = pl.pallas_call(
        functools.partial(_expert_up_kernel, blk=blk),
        grid_spec=pltpu.PrefetchScalarGridSpec(
            num_scalar_prefetch=2,
            grid=(nblk,),
            in_specs=[pl.BlockSpec((1, 1, blk), lambda b, be, nu: (b, 0, 0), memory_space=pltpu.SMEM),
                      pl.BlockSpec((1, 1, blk), lambda b, be, nu: (jnp.minimum(b + 1, nu[0] - 1), 0, 0),
                                   memory_space=pltpu.SMEM),
                      pl.BlockSpec(memory_space=pl.ANY),
                      pl.BlockSpec((1, D, dff), wmap),
                      pl.BlockSpec((1, D, dff), wmap)],
            out_specs=pl.BlockSpec((blk, dff), lambda b, be, nu: (b, 0)),
            scratch_shapes=[pltpu.VMEM((2, blk, Dh), jnp.uint32), pltpu.SemaphoreType.DMA((2,))],
        ),
        out_shape=jax.ShapeDtypeStruct((nblk * blk, dff), BF16),
        compiler_params=_cparams(("arbitrary",)),
        name="experts_up",
    )(blk_e, nused, slot_tok, slot_tok, xn2, wg, wu)
    return pl.pallas_call(
        _expert_down_kernel,
        grid_spec=pltpu.PrefetchScalarGridSpec(
            num_scalar_prefetch=2,
            grid=(nblk,),
            in_specs=[pl.BlockSpec((blk, dff), lambda b, be, nu: (b, 0)),
                      pl.BlockSpec((1, dff, D), wmap)],
            out_specs=pl.BlockSpec((blk, Dh), lambda b, be, nu: (b, 0)),
        ),
        out_shape=jax.ShapeDtypeStruct((nblk * blk, Dh), jnp.uint32),
        compiler_params=_cparams(("arbitrary",)),
        name="experts_down",
    )(blk_e, nused, hid, wd)


def _combine_kernel(d_ref, dn_ref, h_ref, gw_ref, ys_hbm, o_ref, buf, sem, *, tm, nt):
    i = pl.program_id(0)
    slot = i % 2
    half = h_ref.shape[1] // 2

    @pl.when(i == 0)
    def _():
        _gather_start(d_ref, ys_hbm, buf.at[0], sem.at[0], 2 * tm)

    _gather_wait(ys_hbm, buf.at[slot], sem.at[slot], 2 * tm)
    _gather_start(dn_ref, ys_hbm, buf.at[1 - slot], sem.at[1 - slot], 2 * tm)
    lo0, hi0 = _unpack_halves(buf[slot, :tm, :])
    lo1, hi1 = _unpack_halves(buf[slot, tm:, :])
    w0 = gw_ref[:, 0:1]
    w1 = gw_ref[:, 1:2]
    o_ref[:, :half] = h_ref[:, :half] + w0 * lo0 + w1 * lo1
    o_ref[:, half:] = h_ref[:, half:] + w0 * hi0 + w1 * hi1

    @pl.when(i == nt - 1)
    def _():
        _gather_wait(ys_hbm, buf.at[1 - slot], sem.at[1 - slot], 2 * tm)


def _combine(h, ys, dest, gate, tm=128):
    T, D = h.shape
    tm = min(tm, T)
    nt = T // tm
    d3 = dest.reshape(nt, tm, 2).transpose(0, 2, 1).reshape(nt, 1, 2 * tm)
    return pl.pallas_call(
        functools.partial(_combine_kernel, tm=tm, nt=nt),
        grid=(nt,),
        in_specs=[pl.BlockSpec((1, 1, 2 * tm), lambda i: (i, 0, 0), memory_space=pltpu.SMEM),
                  pl.BlockSpec((1, 1, 2 * tm), lambda i: (jnp.minimum(i + 1, nt - 1), 0, 0),
                               memory_space=pltpu.SMEM),
                  pl.BlockSpec((tm, D), lambda i: (i, 0)),
                  pl.BlockSpec((tm, TOP_K_INNER), lambda i: (i, 0)),
                  pl.BlockSpec(memory_space=pl.ANY)],
        out_specs=pl.BlockSpec((tm, D), lambda i: (i, 0)),
        out_shape=jax.ShapeDtypeStruct((T, D), F32),
        scratch_shapes=[pltpu.VMEM((2, 2 * tm, D // 2), jnp.uint32), pltpu.SemaphoreType.DMA((2,))],
        compiler_params=_cparams(("arbitrary",)),
        name="moe_combine",
    )(d3, d3, h, gate, ys)


def _route(logits, blk):
    T = logits.shape[0]
    ng, epg, ne = N_EXPERT_GROUPS, EXPERTS_PER_GROUP, N_EXPERTS
    gp = jax.nn.softmax(logits[:, :ng], axis=-1)
    gsel = jnp.argmax(gp, axis=-1).astype(jnp.int32)
    gprob = jnp.take_along_axis(gp, gsel[:, None], axis=1)[:, 0]
    elog = logits[:, ng:ng + ne].reshape(T, ng, epg)
    elog = jnp.take_along_axis(elog, gsel[:, None, None], axis=1)[:, 0]
    topv, topi = lax.top_k(jax.nn.softmax(elog, axis=-1), TOP_K_INNER)
    gate = topv / jnp.sum(topv, axis=-1, keepdims=True) * gprob[:, None]
    eid = (gsel[:, None] * epg + topi.astype(jnp.int32)).reshape(-1)
    N = T * TOP_K_INNER
    onehot = (eid[:, None] == jnp.arange(ne, dtype=jnp.int32)[None, :]).astype(jnp.int32)
    csum = jnp.cumsum(onehot, axis=0)
    rank = jnp.take_along_axis(csum, eid[:, None], axis=1)[:, 0] - 1
    counts = csum[-1]
    pcounts = ((counts + blk - 1) // blk) * blk
    pends = jnp.cumsum(pcounts)
    pstarts = pends - pcounts
    dest = pstarts[eid] + rank
    nblk = N // blk + ne
    tok = jnp.arange(N, dtype=jnp.int32) // TOP_K_INNER
    slot_tok = jnp.zeros((nblk * blk,), jnp.int32).at[dest].set(tok)
    blk_e = jnp.clip(jnp.searchsorted(pends, jnp.arange(nblk, dtype=jnp.int32) * blk, side='right'),
                     0, ne - 1).astype(jnp.int32)
    nused = (pends[-1] // blk).astype(jnp.int32).reshape(1)
    return (blk_e, nused, slot_tok.reshape(nblk, 1, blk), dest.reshape(T, TOP_K_INNER).astype(jnp.int32),
            gate.astype(F32))


def _moe(h, norm2_g, rw, rb, wg, wu, wd, blk=256):
    xn2, logits = _norm_router(h, norm2_g, rw, rb)
    blk = min(blk, h.shape[0])
    blk_e, nused, slot_tok, dest, gate = _route(logits, blk)
    ys = _experts(xn2, blk_e, nused, slot_tok, wg, wu, wd, blk)
    return _combine(h, ys, dest, gate)


def _layer(x3, lam_init, p):
    nb, L, D = x3.shape
    T = nb * L
    x = x3.reshape(T, D)
    w_in = p['w_in']
    att_w = N_HEADS * DV
    xn = _rmsnorm_cast(x, p['norm1_g'])
    qkn = _proj_qk(xn, w_in, p['q_norm_g'], p['k_norm_g'], 2 * att_w)
    v = _matmul(xn, w_in, 2 * att_w, att_w, BF16, "in_proj_v")
    u = _matmul(xn, w_in, 3 * att_w, w_in.shape[1] - 3 * att_w, F32, "in_proj_u")
    att = _attention(qkn, v, p['lam'], p['attn_sub_g'], p['q_norm_g'], p['k_norm_g'], lam_init, nb, L)
    y = _s5(u, p['s5'](L // S5_CHUNK), nb, L)
    ssm = _glu_norm(y, p['glu_w'], p['glu_b'], p['ssm_out_g'])
    h = _out_proj(att, ssm, p['w_out'], x)
    out = _moe(h, p['norm2_g'], p['rw'], p['rb'], p['wg'], p['wu'], p['wd'])
    return out.reshape(nb, L, D)


def kernel(x_prompt, x_sample, norm1_g, w_in, q_norm_g, k_norm_g, lam_q1, lam_k1, lam_q2, lam_k2, attn_sub_g, ssm_A_re, ssm_A_im, ssm_log_dt, ssm_B_re, ssm_B_im, ssm_C_re, ssm_C_im, ssm_D, glu_w, glu_b, ssm_out_g, w_out, norm2_g, router_group_w, router_group_b, router_expert_w, router_expert_b, exp_w_gate, exp_w_up, exp_w_down):
    depth = w_in.shape[0]

    def layer_params(l):
        lam_init = 0.8 - 0.6 * math.exp(-0.3 * l)
        lam = (jnp.exp(jnp.sum(lam_q1[l].astype(F32) * lam_k1[l].astype(F32)))
               - jnp.exp(jnp.sum(lam_q2[l].astype(F32) * lam_k2[l].astype(F32))) + lam_init)
        D = w_in.shape[1]
        ng, ne = router_group_w.shape[2], router_expert_w.shape[2]
        rw = jnp.zeros((D, ROUTER_PAD), F32)
        rw = rw.at[:, :ng].set(router_group_w[l].astype(F32)).at[:, ng:ng + ne].set(router_expert_w[l].astype(F32))
        rb = jnp.zeros((1, ROUTER_PAD), F32)
        rb = rb.at[0, :ng].set(router_group_b[l].astype(F32)).at[0, ng:ng + ne].set(router_expert_b[l].astype(F32))
        s5_cache = {}

        def s5(nc):
            if nc not in s5_cache:
                s5_cache[nc] = _s5_prep(ssm_A_re[l], ssm_A_im[l], ssm_log_dt[l], ssm_B_re[l], ssm_B_im[l],
                                        ssm_C_re[l], ssm_C_im[l], ssm_D[l], nc)
            return s5_cache[nc]

        return lam_init, dict(
            norm1_g=norm1_g[l], w_in=w_in[l].astype(BF16), q_norm_g=q_norm_g[l], k_norm_g=k_norm_g[l], lam=lam,
            attn_sub_g=attn_sub_g[l], s5=s5, glu_w=glu_w[l].astype(BF16), glu_b=glu_b[l], ssm_out_g=ssm_out_g[l],
            w_out=w_out[l].astype(BF16), norm2_g=norm2_g[l], rw=rw, rb=rb,
            wg=exp_w_gate[l].astype(BF16), wu=exp_w_up[l].astype(BF16), wd=exp_w_down[l].astype(BF16))

    params = [layer_params(l) for l in range(depth)]

    def run(x):
        for lam_init, p in params:
            x = _layer(x, lam_init, p)
        return x

    return (run(x_prompt), run(x_sample))
```

```python
import functools
import math

import jax
import jax.numpy as jnp
from jax import lax
from jax.experimental import pallas as pl
from jax.experimental.pallas import tpu as pltpu

F32 = jnp.float32
BF16 = jnp.bfloat16

RMS_EPS = 1e-6
LOG2E = 1.4426950408889634
LANES = 128
SUBLANES = 8
VMEM_LIMIT = 56 * 1024 * 1024

N_HEADS = 16
DQK = 64
DV = 2 * DQK
SSM_CH = 16
SSM_STATE = 64
S5_CHUNK = 16
N_EXPERT_GROUPS = 4
EXPERTS_PER_GROUP = 8
N_EXPERTS = N_EXPERT_GROUPS * EXPERTS_PER_GROUP
TOP_K_INNER = 2
ROUTER_PAD = LANES


def _cparams(sem):
    return pltpu.CompilerParams(dimension_semantics=sem, vmem_limit_bytes=VMEM_LIMIT)


def _rmsnorm_kernel(x_ref, g_ref, o_ref):
    x = x_ref[...]
    ms = jnp.mean(x * x, axis=-1, keepdims=True)
    o_ref[...] = (x * lax.rsqrt(ms + RMS_EPS) * g_ref[...]).astype(o_ref.dtype)


def _rmsnorm_cast(x, g, tm=512):
    T, D = x.shape
    tm = min(tm, T)
    return pl.pallas_call(
        _rmsnorm_kernel,
        grid=(T // tm,),
        in_specs=[pl.BlockSpec((tm, D), lambda i: (i, 0)),
                  pl.BlockSpec((1, D), lambda i: (0, 0))],
        out_specs=pl.BlockSpec((tm, D), lambda i: (i, 0)),
        out_shape=jax.ShapeDtypeStruct((T, D), BF16),
        compiler_params=_cparams(("parallel",)),
        name="rmsnorm1",
    )(x, g.reshape(1, D).astype(F32))


def _mm_kernel(a_ref, b_ref, o_ref):
    o_ref[...] = jnp.dot(a_ref[...], b_ref[...], preferred_element_type=F32).astype(o_ref.dtype)


def _matmul(a, b, col0, ncols, out_dtype, name, tm=1024, tn=512):
    M, K = a.shape
    tm = min(tm, M)
    tn = min(tn, ncols)
    cb = col0 // tn
    return pl.pallas_call(
        _mm_kernel,
        grid=(M // tm, ncols // tn),
        in_specs=[pl.BlockSpec((tm, K), lambda i, j: (i, 0)),
                  pl.BlockSpec((K, tn), lambda i, j: (0, j + cb))],
        out_specs=pl.BlockSpec((tm, tn), lambda i, j: (i, j)),
        out_shape=jax.ShapeDtypeStruct((M, ncols), out_dtype),
        compiler_params=_cparams(("parallel", "arbitrary")),
        name=name,
    )(a, b)


def _mm_qknorm_kernel(a_ref, b_ref, g_ref, o_ref, *, tn):
    acc = jnp.dot(a_ref[...], b_ref[...], preferred_element_type=F32)
    lane = lax.broadcasted_iota(jnp.int32, (1, LANES), 1)
    left = lane < DQK
    for c in range(tn // LANES):
        x = acc[:, c * LANES:(c + 1) * LANES]
        x2 = x * x
        s_all = jnp.sum(x2, axis=-1, keepdims=True)
        s_left = jnp.sum(jnp.where(left, x2, 0.0), axis=-1, keepdims=True)
        ms = jnp.where(left, s_left, s_all - s_left) * (1.0 / DQK)
        o_ref[:, c * LANES:(c + 1) * LANES] = (
            x * lax.rsqrt(ms + RMS_EPS) * g_ref[:, c * LANES:(c + 1) * LANES]).astype(o_ref.dtype)


def _proj_qk(xn, w, gq, gk, ncols, tm=1024, tn=512):
    M, K = xn.shape
    tm = min(tm, M)
    reps = ncols // (2 * DQK)
    g = jnp.concatenate([jnp.tile(gq.astype(F32) * (DQK ** -0.5 * LOG2E), reps),
                         jnp.tile(gk.astype(F32), reps)]).reshape(1, ncols)
    return pl.pallas_call(
        functools.partial(_mm_qknorm_kernel, tn=tn),
        grid=(M // tm, ncols // tn),
        in_specs=[pl.BlockSpec((tm, K), lambda i, j: (i, 0)),
                  pl.BlockSpec((K, tn), lambda i, j: (0, j)),
                  pl.BlockSpec((1, tn), lambda i, j: (0, j))],
        out_specs=pl.BlockSpec((tm, tn), lambda i, j: (i, j)),
        out_shape=jax.ShapeDtypeStruct((M, ncols), BF16),
        compiler_params=_cparams(("parallel", "arbitrary")),
        name="in_proj_qk",
    )(xn, w, g)


POS_SPLIT = 16


def _alibi_lanes(slopes, tile):
    p = jnp.stack(_split3_bf16(slopes), axis=1)
    r = jnp.arange(tile, dtype=jnp.int32)
    pa = (r & ~(POS_SPLIT - 1)).astype(F32)
    pb = (r & (POS_SPLIT - 1)).astype(F32)
    H = slopes.shape[0]
    pos = jnp.broadcast_to(jnp.stack([pa, pa, pa, pb, pb, pb], axis=1)[None], (H, tile, 6))
    pieces = jnp.broadcast_to(jnp.concatenate([p, p], axis=1)[:, None, :], (H, tile, 6))
    pad = jnp.zeros((H, tile, LANES - 12), F32)
    qaug = jnp.concatenate([pieces, pos, pad], axis=2)
    kaug = jnp.concatenate([pos, -pieces, pad], axis=2)
    return qaug.astype(BF16), jnp.stack([kaug, -kaug], axis=1).astype(BF16), jnp.sum(p, axis=1)


def _attn_kernel(lam_ref, sl_ref, dist_ref, q_ref, k_ref, v_ref, g_ref, qaug_ref, kaug_ref, absd_ref, o_ref,
                 acc_ref, m_ref, s_ref, rm_ref, *, tq, tk, nk, out_scale):
    h = pl.program_id(1)
    qi = pl.program_id(2)
    q0 = qi * tq
    slope = sl_ref[h]
    lam = lam_ref[0]
    q = q_ref[...]
    lane = lax.broadcasted_iota(jnp.int32, (1, LANES), 1)
    zero = jnp.zeros_like(q)
    qs = jnp.concatenate([jnp.where(lane < DQK, q, zero), jnp.where(lane >= DQK, q, zero)], axis=0)
    qaug = qaug_ref[0]
    qcat = jnp.concatenate([qs, jnp.concatenate([qaug, qaug], axis=0)], axis=1)
    ones_col = jnp.broadcast_to(jnp.where(lane == 0, 1.0, 0.0).astype(BF16), (tk, LANES))
    nt = (((1,), (1,)), ((), ()))

    m_ref[...] = jnp.full(m_ref.shape, -1e30, F32)
    acc_ref[...] = jnp.zeros(acc_ref.shape, F32)

    def row_max(sp):
        return jnp.broadcast_to(jnp.max(sp, axis=-1, keepdims=True), (2 * tq, LANES))

    def update(sp, rm, c, start):
        m_old = m_ref[...]
        m_new = jnp.maximum(m_old, rm + c)
        alpha = jnp.exp2(m_old - m_new)
        mu = m_new - c
        p = jnp.exp2(sp - jnp.concatenate([mu] * (tk // LANES), axis=1)).astype(BF16)
        vcat = jnp.concatenate([v_ref[pl.ds(start, tk), :], ones_col], axis=1)
        acc_ref[...] = (jnp.concatenate([alpha, alpha], axis=1) * acc_ref[...]
                        + jnp.dot(p, vcat, preferred_element_type=F32))
        m_ref[...] = m_new

    dist = dist_ref[h]
    lo = jnp.maximum(q0 - dist, 0) // tk
    hi = jnp.minimum((q0 + tq - 1 + dist) // tk + 1, nk)
    n_left = qi - lo
    n_side = n_left + (hi - qi - 1)

    def tile_of(t):
        t = jnp.minimum(t, n_side - 1)
        return jnp.clip(jnp.where(t < n_left, lo + t, qi + 1 + t - n_left), 0, nk - 1), t < n_left

    def scores(slot, t):
        ki, is_left = tile_of(t)
        start = pl.multiple_of(ki * tk, tk)
        kaug = kaug_ref[0, jnp.where(is_left, 0, 1)]
        kc = jnp.concatenate([k_ref[pl.ds(start, tk), :], kaug], axis=1)
        sp = lax.dot_general(qcat, kc, nt, preferred_element_type=F32)
        s_ref[slot] = sp
        rm_ref[slot] = row_max(sp)

    def consume(slot, t):
        ki, is_left = tile_of(t)
        dist0 = jnp.where(is_left, q0 - ki * tk, ki * tk - q0)
        update(s_ref[slot], rm_ref[slot], -slope * lax.convert_element_type(dist0, F32),
               pl.multiple_of(ki * tk, tk))

    def pair(t0):
        scores(1, t0 + 1)
        consume(0, t0)
        scores(0, t0 + 2)
        consume(1, t0 + 1)

    def quad(i, carry):
        pair(4 * i)
        pair(4 * i + 2)
        return carry

    scores(0, 0)
    start_d = pl.multiple_of(q0, tk)
    bias = absd_ref[...] * (-slope)
    sp_d = (lax.dot_general(qs, k_ref[pl.ds(start_d, tk), :], nt, preferred_element_type=F32)
            + jnp.concatenate([bias, bias], axis=0))
    update(sp_d, row_max(sp_d), 0.0, start_d)
    n_quads = n_side // 4
    lax.fori_loop(0, n_quads, quad, 0)
    rem = n_side - 4 * n_quads

    @pl.when(rem >= 2)
    def _():
        pair(4 * n_quads)

    @pl.when(rem % 2 == 1)
    def _():
        consume(0, n_side - 1)

    acc = acc_ref[...]
    o = acc[:tq, :DV] / acc[:tq, DV:DV + 1] - lam * (acc[tq:, :DV] / acc[tq:, DV:DV + 1])
    ms = jnp.mean(o * o, axis=-1, keepdims=True)
    o_ref[...] = (o * lax.rsqrt(ms + RMS_EPS) * (g_ref[...] * out_scale)).astype(o_ref.dtype)


def _split3_bf16(x):
    p0 = x.astype(BF16).astype(F32)
    p1 = (x - p0).astype(BF16).astype(F32)
    p2 = (x - p0 - p1).astype(BF16).astype(F32)
    return p0, p1, p2


F32_EXP_ZERO = 104.0


def _zero_weight_distance(gq, gk, slopes_nat, L):
    bound = 8.0 * jnp.max(jnp.abs(gq.astype(F32))) * jnp.max(jnp.abs(gk.astype(F32))) * 1.02
    d = jnp.ceil((F32_EXP_ZERO + 2.0 * bound) / slopes_nat) + 1.0
    return jnp.clip(d, 1.0, float(L)).astype(jnp.int32)


def _attention(qkn, vu, lam, sub_g, gq, gk, lam_init, nb, L, tile=512):
    T = qkn.shape[0]
    tq = tk = min(tile, L)
    nq = L // tq
    slopes_nat = jnp.exp2(-8.0 * jnp.arange(1, N_HEADS + 1, dtype=F32) / N_HEADS)
    qaug, kaug, sl = _alibi_lanes(slopes_nat * LOG2E, tq)
    idx = jnp.arange(tq, dtype=jnp.int32)
    absd = jnp.abs(idx[:, None] - idx[None, :]).astype(F32)
    dist = _zero_weight_distance(gq, gk, slopes_nat, L)
    kern = functools.partial(_attn_kernel, tq=tq, tk=tk, nk=L // tk, out_scale=1.0 - lam_init)
    return pl.pallas_call(
        kern,
        grid=(nb, N_HEADS, nq),
        in_specs=[pl.BlockSpec(memory_space=pltpu.SMEM),
                  pl.BlockSpec(memory_space=pltpu.SMEM),
                  pl.BlockSpec(memory_space=pltpu.SMEM),
                  pl.BlockSpec((tq, LANES), lambda b, h, i: (b * nq + i, h)),
                  pl.BlockSpec((L, LANES), lambda b, h, i: (b, N_HEADS + h)),
                  pl.BlockSpec((L, LANES), lambda b, h, i: (b, h)),
                  pl.BlockSpec((1, DV), lambda b, h, i: (0, 0)),
                  pl.BlockSpec((1, tq, LANES), lambda b, h, i: (h, 0, 0)),
                  pl.BlockSpec((1, 2, tk, LANES), lambda b, h, i: (h, 0, 0, 0)),
                  pl.BlockSpec((tq, tk), lambda b, h, i: (0, 0))],
        out_specs=pl.BlockSpec((tq, LANES), lambda b, h, i: (b * nq + i, h)),
        out_shape=jax.ShapeDtypeStruct((T, N_HEADS * DV), BF16),
        scratch_shapes=[pltpu.VMEM((2 * tq, 2 * LANES), F32), pltpu.VMEM((2 * tq, LANES), F32),
                        pltpu.VMEM((2, 2 * tq, tk), F32), pltpu.VMEM((2, 2 * tq, LANES), F32)],
        compiler_params=_cparams(("parallel", "parallel", "arbitrary")),
        name="diff_attn",
    )(lam.reshape(1).astype(F32), sl, dist, qkn, qkn, vu, sub_g.reshape(1, DV).astype(F32), qaug, kaug, absd)


def _s5_prep(A_re, A_im, log_dt, B_re, B_im, C_re, C_im, d_skip, nc):
    tc = S5_CHUNK
    hp = lax.Precision.HIGHEST
    A_re, A_im, log_dt = A_re.astype(F32), A_im.astype(F32), log_dt.astype(F32)
    B_re, B_im, C_re, C_im = B_re.astype(F32), B_im.astype(F32), C_re.astype(F32), C_im.astype(F32)
    G = A_re.shape[1]
    dt = jnp.exp(log_dt)[..., None]
    ang = A_im * dt
    mag = jnp.exp(A_re * dt)
    lr, li = mag * jnp.cos(ang), mag * jnp.sin(ang)
    den = A_re * A_re + A_im * A_im
    cr = ((lr - 1.0) * A_re + li * A_im) / den
    ci = (li * A_re - (lr - 1.0) * A_im) / den
    bbr = cr[..., None] * B_re - ci[..., None] * B_im
    bbi = cr[..., None] * B_im + ci[..., None] * B_re

    def lam_pow(n):
        nn = n.astype(F32)[:, None, None, None]
        pm = jnp.exp(A_re * dt * nn)
        pa = ang * nn
        return pm * jnp.cos(pa), pm * jnp.sin(pa)

    pr, pi = lam_pow(jnp.arange(tc + 1))
    lbr = pr[..., None] * bbr - pi[..., None] * bbi
    lbi = pr[..., None] * bbi + pi[..., None] * bbr
    kk = (jnp.einsum('dghp,ndgpb->ndghb', C_re, lbr, precision=hp)
          - jnp.einsum('dghp,ndgpb->ndghb', C_im, lbi, precision=hp))
    j = jnp.arange(tc)[:, None]
    t = jnp.arange(tc)[None, :]
    kf = jnp.where((t >= j)[:, :, None, None, None], kk[jnp.clip(t - j, 0, tc), 0], 0.0)
    kb = jnp.where((j >= t)[:, :, None, None, None], kk[jnp.clip(j - t, 0, tc), 1], 0.0)
    eye_t = (j == t).astype(F32)[:, :, None, None, None]
    eye_c = jnp.eye(SSM_CH, dtype=F32)[None, None, None]
    dd = eye_t * eye_c * d_skip.astype(F32)[None, None, :, :, None]
    ty = (kf + kb + dd).transpose(2, 0, 4, 1, 3).reshape(G, tc * SSM_CH, tc * SSM_CH)

    def state_cols(x):
        return x.transpose(1, 0, 3, 2).reshape(G, tc * SSM_CH, SSM_STATE)

    nf = tc - 1 - jnp.arange(tc)
    nbk = jnp.arange(tc)
    w1 = jnp.concatenate([ty, state_cols(lbr[nf, 0]), state_cols(lbi[nf, 0]),
                          state_cols(lbr[nbk, 1]), state_cols(lbi[nbk, 1])], axis=-1)

    def carry_rows(n, d):
        a = C_re[d][None] * pr[n, d][:, :, None, :] - C_im[d][None] * pi[n, d][:, :, None, :]
        bco = C_re[d][None] * pi[n, d][:, :, None, :] + C_im[d][None] * pr[n, d][:, :, None, :]
        to_rows = lambda x: x.transpose(1, 3, 0, 2).reshape(G, SSM_STATE, tc * SSM_CH)
        return to_rows(a), to_rows(-bco)

    fa, fb = carry_rows(jnp.arange(tc) + 1, 0)
    ba, bb = carry_rows(tc - jnp.arange(tc), 1)
    w2 = jnp.concatenate([fa, fb, ba, bb], axis=1)

    nsteps = max(1, int(math.ceil(math.log2(nc))))
    sr, si = lam_pow(tc * (2 ** jnp.arange(nsteps)))
    rows = jnp.stack([jnp.concatenate([sr, sr], -1), jnp.concatenate([-si, si], -1)], axis=2)
    coef = rows.transpose(3, 0, 1, 2, 4).reshape(G, nsteps * 4, LANES)
    return w1.astype(BF16), w2.astype(BF16), coef, nsteps


GROUPS_PER_TILE = LANES // SSM_CH


def _granule_transpose(arrs):
    lane = lax.broadcasted_iota(jnp.int32, (1, LANES), 1)
    a = list(arrs)
    for dist, shift in ((4, 4 * SSM_CH), (2, 2 * SSM_CH), (1, SSM_CH)):
        keep_lo = (lane & (2 * shift - 1)) < shift
        new = list(a)
        for i in range(GROUPS_PER_TILE):
            if (i // dist) % 2 == 0:
                lo, hi = a[i], a[i + dist]
                new[i] = jnp.where(keep_lo, lo, pltpu.roll(hi, shift, axis=1))
                new[i + dist] = jnp.where(keep_lo, pltpu.roll(lo, LANES - shift, axis=1), hi)
        a = new
    return a


def _s5_kernel(u_ref, w1_ref, w2_ref, c_ref, o_ref, ug_ref, yg_ref, *, nc, nsteps):
    ny = S5_CHUNK * SSM_CH
    halves = S5_CHUNK // GROUPS_PER_TILE
    for hf in range(halves):
        xs = [u_ref[pl.ds(GROUPS_PER_TILE * hf + j, nc, stride=S5_CHUNK), :] for j in range(GROUPS_PER_TILE)]
        for g, x in enumerate(_granule_transpose(xs)):
            ug_ref[g, :, hf * LANES:(hf + 1) * LANES] = x.astype(BF16)

    row = lax.broadcasted_iota(jnp.int32, (nc, 1), 0)

    def shifted(x, d, fwd):
        if d % SUBLANES == 0:
            pad = jnp.zeros((d, x.shape[1]), x.dtype)
            return jnp.concatenate([pad, x[:nc - d]] if fwd else [x[d:], pad], axis=0)
        if fwd:
            return jnp.where(row >= d, pltpu.roll(x, d, axis=0), 0.0)
        return jnp.where(row < nc - d, pltpu.roll(x, nc - d, axis=0), 0.0)

    def group(g, carry):
        r = jnp.dot(ug_ref[g], w1_ref[g], preferred_element_type=F32)

        def carry_in(x, dirn):
            s = shifted(x, 1, dirn == 0)
            for st in range(nsteps):
                d = 2 ** st
                if d >= nc:
                    break
                sh = shifted(s, d, dirn == 0)
                a = c_ref[g, 4 * st + 2 * dirn:4 * st + 2 * dirn + 1, :]
                b = c_ref[g, 4 * st + 2 * dirn + 1:4 * st + 2 * dirn + 2, :]
                s = s + a * sh + b * pltpu.roll(sh, SSM_STATE, axis=1)
            return s

        sf = carry_in(r[:, ny:ny + LANES], 0)
        sb = carry_in(r[:, ny + LANES:ny + 2 * LANES], 1)
        s = jnp.concatenate([sf, sb], axis=1).astype(BF16)
        yg_ref[g] = r[:, :ny] + jnp.dot(s, w2_ref[g], preferred_element_type=F32)
        return carry

    lax.fori_loop(0, GROUPS_PER_TILE, group, 0)

    for hf in range(halves):
        ys = [yg_ref[g, :, hf * LANES:(hf + 1) * LANES] for g in range(GROUPS_PER_TILE)]
        for j, x in enumerate(_granule_transpose(ys)):
            o_ref[pl.ds(GROUPS_PER_TILE * hf + j, nc, stride=S5_CHUNK), :] = x


def _s5(u, prep, nb, L):
    w1, w2, coef, nsteps = prep
    T, W = u.shape
    gt = GROUPS_PER_TILE
    nc = L // S5_CHUNK
    ny = S5_CHUNK * SSM_CH
    return pl.pallas_call(
        functools.partial(_s5_kernel, nc=nc, nsteps=nsteps),
        grid=(nb, W // LANES),
        in_specs=[pl.BlockSpec((L, LANES), lambda b, t: (b, t), pipeline_mode=pl.Buffered(1)),
                  pl.BlockSpec((gt, ny, 2 * ny), lambda b, t: (t, 0, 0)),
                  pl.BlockSpec((gt, ny, ny), lambda b, t: (t, 0, 0)),
                  pl.BlockSpec((gt, 4 * nsteps, LANES), lambda b, t: (t, 0, 0))],
        out_specs=pl.BlockSpec((L, LANES), lambda b, t: (b, t), pipeline_mode=pl.Buffered(1)),
        out_shape=jax.ShapeDtypeStruct((T, W), F32),
        scratch_shapes=[pltpu.VMEM((gt, nc, ny), BF16), pltpu.VMEM((gt, nc, ny), F32)],
        compiler_params=_cparams(("parallel", "parallel")),
        name="s5_chunked",
    )(u, w1, w2, coef)


def _glu_kernel(y_ref, w_ref, b_ref, g_ref, o_ref):
    y = y_ref[...]
    c0 = math.sqrt(2.0 / math.pi)
    gl = 0.5 * y * (1.0 + jnp.tanh(c0 * (y + 0.044715 * (y * y * y))))
    z = jnp.dot(gl.astype(BF16), w_ref[...], preferred_element_type=F32) + b_ref[...]
    s = gl * (1.0 / (1.0 + jnp.exp(-z)))
    ms = jnp.mean(s * s, axis=-1, keepdims=True)
    o_ref[...] = (s * lax.rsqrt(ms + RMS_EPS) * g_ref[...]).astype(o_ref.dtype)


def _glu_norm(y, w, b, g, tm=512):
    T, W = y.shape
    tm = min(tm, T)
    return pl.pallas_call(
        _glu_kernel,
        grid=(T // tm,),
        in_specs=[pl.BlockSpec((tm, W), lambda i: (i, 0)),
                  pl.BlockSpec((W, W), lambda i: (0, 0)),
                  pl.BlockSpec((1, W), lambda i: (0, 0)),
                  pl.BlockSpec((1, W), lambda i: (0, 0))],
        out_specs=pl.BlockSpec((tm, W), lambda i: (i, 0)),
        out_shape=jax.ShapeDtypeStruct((T, W), BF16),
        compiler_params=_cparams(("parallel",)),
        name="glu_norm",
    )(y, w, b.reshape(1, W).astype(F32), g.reshape(1, W).astype(F32))


def _out_proj_kernel(a_ref, s_ref, wa_ref, ws_ref, x_ref, o_ref):
    acc = jnp.dot(a_ref[...], wa_ref[...], preferred_element_type=F32)
    acc = acc + jnp.dot(s_ref[...], ws_ref[...], preferred_element_type=F32)
    o_ref[...] = x_ref[...] + acc


def _out_proj(att, ssm, w, x, tm=1024, tn=512):
    T, Wa = att.shape
    D = x.shape[1]
    tm = min(tm, T)
    tn = min(tn, D)
    rb = Wa // Wa
    return pl.pallas_call(
        _out_proj_kernel,
        grid=(T // tm, D // tn),
        in_specs=[pl.BlockSpec((tm, Wa), lambda i, j: (i, 0)),
                  pl.BlockSpec((tm, Wa), lambda i, j: (i, 0)),
                  pl.BlockSpec((Wa, tn), lambda i, j: (0, j)),
                  pl.BlockSpec((Wa, tn), lambda i, j: (rb, j)),
                  pl.BlockSpec((tm, tn), lambda i, j: (i, j))],
        out_specs=pl.BlockSpec((tm, tn), lambda i, j: (i, j)),
        out_shape=jax.ShapeDtypeStruct((T, D), F32),
        compiler_params=_cparams(("parallel", "arbitrary")),
        name="out_proj",
    )(att, ssm, w, w, x)


def _router_kernel(h_ref, g_ref, w_ref, b_ref, xn_ref, lg_ref):
    x = h_ref[...]
    ms = jnp.mean(x * x, axis=-1, keepdims=True)
    xn = x * lax.rsqrt(ms + RMS_EPS) * g_ref[...]
    xn_ref[...] = _pack_halves(xn)
    lg_ref[...] = jnp.dot(xn, w_ref[...], preferred_element_type=F32,
                          precision=lax.Precision.HIGHEST) + b_ref[...]


def _norm_router(h, g, rw, rb, tm=256):
    T, D = h.shape
    tm = min(tm, T)
    return pl.pallas_call(
        _router_kernel,
        grid=(T // tm,),
        in_specs=[pl.BlockSpec((tm, D), lambda i: (i, 0)),
                  pl.BlockSpec((1, D), lambda i: (0, 0)),
                  pl.BlockSpec((D, ROUTER_PAD), lambda i: (0, 0)),
                  pl.BlockSpec((1, ROUTER_PAD), lambda i: (0, 0))],
        out_specs=[pl.BlockSpec((tm, D // 2), lambda i: (i, 0)),
                   pl.BlockSpec((tm, ROUTER_PAD), lambda i: (i, 0))],
        out_shape=[jax.ShapeDtypeStruct((T, D // 2), jnp.uint32), jax.ShapeDtypeStruct((T, ROUTER_PAD), F32)],
        compiler_params=_cparams(("parallel",)),
        name="norm_router",
    )(h, g.reshape(1, D).astype(F32), rw, rb)


def _pack_halves(x):
    half = x.shape[1] // 2
    lo = pltpu.bitcast(x[:, :half].astype(BF16).astype(F32), jnp.uint32)
    hi = pltpu.bitcast(x[:, half:].astype(BF16).astype(F32), jnp.uint32)
    return hi | (lo >> 16)


def _unpack_halves(w):
    lo = pltpu.bitcast(w << 16, F32)
    hi = pltpu.bitcast(w & jnp.uint32(0xFFFF0000), F32)
    return lo, hi


def _row_copy(src_hbm, src_row, dst, dst_row, sem):
    return pltpu.make_async_copy(src_hbm.at[pl.ds(src_row, 1), :], dst.at[pl.ds(dst_row, 1), :], sem)


def _gather_start(idx_ref, src_hbm, dst, sem, n):
    for r in range(n):
        _row_copy(src_hbm, idx_ref[0, 0, r], dst, r, sem).start()


def _gather_wait(src_hbm, dst, sem, n):
    for r in range(n):
        _row_copy(src_hbm, 0, dst, r, sem).wait()


def _expert_up_kernel(be_ref, nu_ref, tok_ref, tokn_ref, x_hbm, wg_ref, wu_ref, h_ref, xbuf, sem, *, blk):
    b = pl.program_id(0)
    nused = nu_ref[0]
    slot = b % 2

    @pl.when(b == 0)
    def _():
        _gather_start(tok_ref, x_hbm, xbuf.at[0], sem.at[0], blk)

    @pl.when(b < nused)
    def _():
        _gather_wait(x_hbm, xbuf.at[slot], sem.at[slot], blk)
        _gather_start(tokn_ref, x_hbm, xbuf.at[1 - slot], sem.at[1 - slot], blk)
        lo, hi = _unpack_halves(xbuf[slot])
        x = jnp.concatenate([lo, hi], axis=1).astype(BF16)
        g = jnp.dot(x, wg_ref[0], preferred_element_type=F32)
        u = jnp.dot(x, wu_ref[0], preferred_element_type=F32)
        h_ref[...] = (g * (1.0 / (1.0 + jnp.exp(-g))) * u).astype(h_ref.dtype)

    @pl.when(b >= nused)
    def _():
        h_ref[...] = jnp.zeros_like(h_ref)

    @pl.when(b == nused)
    def _():
        _gather_wait(x_hbm, xbuf.at[slot], sem.at[slot], blk)


def _expert_down_kernel(be_ref, nu_ref, h_ref, wd_ref, o_ref):
    live = pl.program_id(0) < nu_ref[0]

    @pl.when(live)
    def _():
        o_ref[...] = _pack_halves(jnp.dot(h_ref[...], wd_ref[0], preferred_element_type=F32))

    @pl.when(jnp.logical_not(live))
    def _():
        o_ref[...] = jnp.zeros_like(o_ref)


def _experts(xn2, blk_e, nused, slot_tok, wg, wu, wd, blk):
    T, Dh = xn2.shape
    D = 2 * Dh
    nblk = slot_tok.shape[0]
    dff = wg.shape[2]

    def wmap(b, be, nu):
        return (be[jnp.minimum(b, nu[0] - 1)], 0, 0)

    hid = pl.pallas_call(
        functools.partial(_expert_up_kernel, blk=blk),
        grid_spec=pltpu.PrefetchScalarGridSpec(
            num_scalar_prefetch=2,
            grid=(nblk,),
            in_specs=[pl.BlockSpec((1, 1, blk), lambda b, be, nu: (b, 0, 0), memory_space=pltpu.SMEM),
                      pl.BlockSpec((1, 1, blk), lambda b, be, nu: (jnp.minimum(b + 1, nu[0] - 1), 0, 0),
                                   memory_space=pltpu.SMEM),
                      pl.BlockSpec(memory_space=pl.ANY),
                      pl.BlockSpec((1, D, dff), wmap),
                      pl.BlockSpec((1, D, dff), wmap)],
            out_specs=pl.BlockSpec((blk, dff), lambda b, be, nu: (b, 0)),
            scratch_shapes=[pltpu.VMEM((2, blk, Dh), jnp.uint32), pltpu.SemaphoreType.DMA((2,))],
        ),
        out_shape=jax.ShapeDtypeStruct((nblk * blk, dff), BF16),
        compiler_params=_cparams(("arbitrary",)),
        name="experts_up",
    )(blk_e, nused, slot_tok, slot_tok, xn2, wg, wu)
    return pl.pallas_call(
        _expert_down_kernel,
        grid_spec=pltpu.PrefetchScalarGridSpec(
            num_scalar_prefetch=2,
            grid=(nblk,),
            in_specs=[pl.BlockSpec((blk, dff), lambda b, be, nu: (b, 0)),
                      pl.BlockSpec((1, dff, D), wmap)],
            out_specs=pl.BlockSpec((blk, Dh), lambda b, be, nu: (b, 0)),
        ),
        out_shape=jax.ShapeDtypeStruct((nblk * blk, Dh), jnp.uint32),
        compiler_params=_cparams(("arbitrary",)),
        name="experts_down",
    )(blk_e, nused, hid, wd)


def _combine_kernel(d_ref, dn_ref, h_ref, gw_ref, ys_hbm, o_ref, buf, sem, *, tm, nt):
    i = pl.program_id(0)
    slot = i % 2
    half = h_ref.shape[1] // 2

    @pl.when(i == 0)
    def _():
        _gather_start(d_ref, ys_hbm, buf.at[0], sem.at[0], 2 * tm)

    _gather_wait(ys_hbm, buf.at[slot], sem.at[slot], 2 * tm)
    _gather_start(dn_ref, ys_hbm, buf.at[1 - slot], sem.at[1 - slot], 2 * tm)
    lo0, hi0 = _unpack_halves(buf[slot, :tm, :])
    lo1, hi1 = _unpack_halves(buf[slot, tm:, :])
    w0 = gw_ref[:, 0:1]
    w1 = gw_ref[:, 1:2]
    o_ref[:, :half] = h_ref[:, :half] + w0 * lo0 + w1 * lo1
    o_ref[:, half:] = h_ref[:, half:] + w0 * hi0 + w1 * hi1

    @pl.when(i == nt - 1)
    def _():
        _gather_wait(ys_hbm, buf.at[1 - slot], sem.at[1 - slot], 2 * tm)


def _combine(h, ys, dest, gate, tm=128):
    T, D = h.shape
    tm = min(tm, T)
    nt = T // tm
    d3 = dest.reshape(nt, tm, 2).transpose(0, 2, 1).reshape(nt, 1, 2 * tm)
    return pl.pallas_call(
        functools.partial(_combine_kernel, tm=tm, nt=nt),
        grid=(nt,),
        in_specs=[pl.BlockSpec((1, 1, 2 * tm), lambda i: (i, 0, 0), memory_space=pltpu.SMEM),
                  pl.BlockSpec((1, 1, 2 * tm), lambda i: (jnp.minimum(i + 1, nt - 1), 0, 0),
                               memory_space=pltpu.SMEM),
                  pl.BlockSpec((tm, D), lambda i: (i, 0)),
                  pl.BlockSpec((tm, TOP_K_INNER), lambda i: (i, 0)),
                  pl.BlockSpec(memory_space=pl.ANY)],
        out_specs=pl.BlockSpec((tm, D), lambda i: (i, 0)),
        out_shape=jax.ShapeDtypeStruct((T, D), F32),
        scratch_shapes=[pltpu.VMEM((2, 2 * tm, D // 2), jnp.uint32), pltpu.SemaphoreType.DMA((2,))],
        compiler_params=_cparams(("arbitrary",)),
        name="moe_combine",
    )(d3, d3, h, gate, ys)


def _route(logits, blk):
    T = logits.shape[0]
    ng, epg, ne = N_EXPERT_GROUPS, EXPERTS_PER_GROUP, N_EXPERTS
    gp = jax.nn.softmax(logits[:, :ng], axis=-1)
    gsel = jnp.argmax(gp, axis=-1).astype(jnp.int32)
    gprob = jnp.take_along_axis(gp, gsel[:, None], axis=1)[:, 0]
    elog = logits[:, ng:ng + ne].reshape(T, ng, epg)
    elog = jnp.take_along_axis(elog, gsel[:, None, None], axis=1)[:, 0]
    topv, topi = lax.top_k(jax.nn.softmax(elog, axis=-1), TOP_K_INNER)
    gate = topv / jnp.sum(topv, axis=-1, keepdims=True) * gprob[:, None]
    eid = (gsel[:, None] * epg + topi.astype(jnp.int32)).reshape(-1)
    N = T * TOP_K_INNER
    onehot = (eid[:, None] == jnp.arange(ne, dtype=jnp.int32)[None, :]).astype(jnp.int32)
    csum = jnp.cumsum(onehot, axis=0)
    rank = jnp.take_along_axis(csum, eid[:, None], axis=1)[:, 0] - 1
    counts = csum[-1]
    pcounts = ((counts + blk - 1) // blk) * blk
    pends = jnp.cumsum(pcounts)
    pstarts = pends - pcounts
    dest = pstarts[eid] + rank
    nblk = N // blk + ne
    tok = jnp.arange(N, dtype=jnp.int32) // TOP_K_INNER
    slot_tok = jnp.zeros((nblk * blk,), jnp.int32).at[dest].set(tok)
    blk_e = jnp.clip(jnp.searchsorted(pends, jnp.arange(nblk, dtype=jnp.int32) * blk, side='right'),
                     0, ne - 1).astype(jnp.int32)
    nused = (pends[-1] // blk).astype(jnp.int32).reshape(1)
    return (blk_e, nused, slot_tok.reshape(nblk, 1, blk), dest.reshape(T, TOP_K_INNER).astype(jnp.int32),
            gate.astype(F32))


def _moe(h, norm2_g, rw, rb, wg, wu, wd, blk=256):
    xn2, logits = _norm_router(h, norm2_g, rw, rb)
    blk = min(blk, h.shape[0])
    blk_e, nused, slot_tok, dest, gate = _route(logits, blk)
    ys = _experts(xn2, blk_e, nused, slot_tok, wg, wu, wd, blk)
    return _combine(h, ys, dest, gate)


def _layer(x3, lam_init, p):
    nb, L, D = x3.shape
    T = nb * L
    x = x3.reshape(T, D)
    w_in = p['w_in']
    att_w = N_HEADS * DV
    xn = _rmsnorm_cast(x, p['norm1_g'])
    qkn = _proj_qk(xn, w_in, p['q_norm_g'], p['k_norm_g'], 2 * att_w)
    v = _matmul(xn, w_in, 2 * att_w, att_w, BF16, "in_proj_v")
    u = _matmul(xn, w_in, 3 * att_w, w_in.shape[1] - 3 * att_w, F32, "in_proj_u")
    att = _attention(qkn, v, p['lam'], p['attn_sub_g'], p['q_norm_g'], p['k_norm_g'], lam_init, nb, L)
    y = _s5(u, p['s5'](L // S5_CHUNK), nb, L)
    ssm = _glu_norm(y, p['glu_w'], p['glu_b'], p['ssm_out_g'])
    h = _out_proj(att, ssm, p['w_out'], x)
    out = _moe(h, p['norm2_g'], p['rw'], p['rb'], p['wg'], p['wu'], p['wd'])
    return out.reshape(nb, L, D)


def kernel(x_prompt, x_sample, norm1_g, w_in, q_norm_g, k_norm_g, lam_q1, lam_k1, lam_q2, lam_k2, attn_sub_g, ssm_A_re, ssm_A_im, ssm_log_dt, ssm_B_re, ssm_B_im, ssm_C_re, ssm_C_im, ssm_D, glu_w, glu_b, ssm_out_g, w_out, norm2_g, router_group_w, router_group_b, router_expert_w, router_expert_b, exp_w_gate, exp_w_up, exp_w_down):
    depth = w_in.shape[0]

    def layer_params(l):
        lam_init = 0.8 - 0.6 * math.exp(-0.3 * l)
        lam = (jnp.exp(jnp.sum(lam_q1[l].astype(F32) * lam_k1[l].astype(F32)))
               - jnp.exp(jnp.sum(lam_q2[l].astype(F32) * lam_k2[l].astype(F32))) + lam_init)
        D = w_in.shape[1]
        ng, ne = router_group_w.shape[2], router_expert_w.shape[2]
        rw = jnp.zeros((D, ROUTER_PAD), F32)
        rw = rw.at[:, :ng].set(router_group_w[l].astype(F32)).at[:, ng:ng + ne].set(router_expert_w[l].astype(F32))
        rb = jnp.zeros((1, ROUTER_PAD), F32)
        rb = rb.at[0, :ng].set(router_group_b[l].astype(F32)).at[0, ng:ng + ne].set(router_expert_b[l].astype(F32))
        s5_cache = {}

        def s5(nc):
            if nc not in s5_cache:
                s5_cache[nc] = _s5_prep(ssm_A_re[l], ssm_A_im[l], ssm_log_dt[l], ssm_B_re[l], ssm_B_im[l],
                                        ssm_C_re[l], ssm_C_im[l], ssm_D[l], nc)
            return s5_cache[nc]

        return lam_init, dict(
            norm1_g=norm1_g[l], w_in=w_in[l].astype(BF16), q_norm_g=q_norm_g[l], k_norm_g=k_norm_g[l], lam=lam,
            attn_sub_g=attn_sub_g[l], s5=s5, glu_w=glu_w[l].astype(BF16), glu_b=glu_b[l], ssm_out_g=ssm_out_g[l],
            w_out=w_out[l].astype(BF16), norm2_g=norm2_g[l], rw=rw, rb=rb,
            wg=exp_w_gate[l].astype(BF16), wu=exp_w_up[l].astype(BF16), wd=exp_w_down[l].astype(BF16))

    params = [layer_params(l) for l in range(depth)]

    def run(x):
        for lam_init, p in params:
            x = _layer(x, lam_init, p)
        return x

    return (run(x_prompt), run(x_sample))
```

```python
import functools
import math

import jax
import jax.numpy as jnp
from jax import lax
from jax.experimental import pallas as pl
from jax.experimental.pallas import tpu as pltpu

F32 = jnp.float32
BF16 = jnp.bfloat16

RMS_EPS = 1e-6
LOG2E = 1.4426950408889634
LANES = 128
SUBLANES = 8
VMEM_LIMIT = 56 * 1024 * 1024

N_HEADS = 16
DQK = 64
DV = 2 * DQK
SSM_CH = 16
SSM_STATE = 64
S5_CHUNK = 16
N_EXPERT_GROUPS = 4
EXPERTS_PER_GROUP = 8
N_EXPERTS = N_EXPERT_GROUPS * EXPERTS_PER_GROUP
TOP_K_INNER = 2
ROUTER_PAD = LANES


def _cparams(sem):
    return pltpu.CompilerParams(dimension_semantics=sem, vmem_limit_bytes=VMEM_LIMIT)


def _rmsnorm_kernel(x_ref, g_ref, o_ref):
    x = x_ref[...]
    ms = jnp.mean(x * x, axis=-1, keepdims=True)
    o_ref[...] = (x * lax.rsqrt(ms + RMS_EPS) * g_ref[...]).astype(o_ref.dtype)


def _rmsnorm_cast(x, g, tm=512):
    T, D = x.shape
    tm = min(tm, T)
    return pl.pallas_call(
        _rmsnorm_kernel,
        grid=(T // tm,),
        in_specs=[pl.BlockSpec((tm, D), lambda i: (i, 0)),
                  pl.BlockSpec((1, D), lambda i: (0, 0))],
        out_specs=pl.BlockSpec((tm, D), lambda i: (i, 0)),
        out_shape=jax.ShapeDtypeStruct((T, D), BF16),
        compiler_params=_cparams(("parallel",)),
        name="rmsnorm1",
    )(x, g.reshape(1, D).astype(F32))


def _mm_kernel(a_ref, b_ref, o_ref):
    o_ref[...] = jnp.dot(a_ref[...], b_ref[...], preferred_element_type=F32).astype(o_ref.dtype)


def _matmul(a, b, col0, ncols, out_dtype, name, tm=1024, tn=512):
    M, K = a.shape
    tm = min(tm, M)
    tn = min(tn, ncols)
    cb = col0 // tn
    return pl.pallas_call(
        _mm_kernel,
        grid=(M // tm, ncols // tn),
        in_specs=[pl.BlockSpec((tm, K), lambda i, j: (i, 0)),
                  pl.BlockSpec((K, tn), lambda i, j: (0, j + cb))],
        out_specs=pl.BlockSpec((tm, tn), lambda i, j: (i, j)),
        out_shape=jax.ShapeDtypeStruct((M, ncols), out_dtype),
        compiler_params=_cparams(("parallel", "arbitrary")),
        name=name,
    )(a, b)


def _mm_qknorm_kernel(a_ref, b_ref, g_ref, o_ref, *, tn):
    acc = jnp.dot(a_ref[...], b_ref[...], preferred_element_type=F32)
    lane = lax.broadcasted_iota(jnp.int32, (1, LANES), 1)
    left = lane < DQK
    for c in range(tn // LANES):
        x = acc[:, c * LANES:(c + 1) * LANES]
        x2 = x * x
        s_all = jnp.sum(x2, axis=-1, keepdims=True)
        s_left = jnp.sum(jnp.where(left, x2, 0.0), axis=-1, keepdims=True)
        ms = jnp.where(left, s_left, s_all - s_left) * (1.0 / DQK)
        o_ref[:, c * LANES:(c + 1) * LANES] = (
            x * lax.rsqrt(ms + RMS_EPS) * g_ref[:, c * LANES:(c + 1) * LANES]).astype(o_ref.dtype)


def _proj_qk(xn, w, gq, gk, ncols, tm=1024, tn=512):
    M, K = xn.shape
    tm = min(tm, M)
    reps = ncols // (2 * DQK)
    g = jnp.concatenate([jnp.tile(gq.astype(F32) * (DQK ** -0.5 * LOG2E), reps),
                         jnp.tile(gk.astype(F32), reps)]).reshape(1, ncols)
    return pl.pallas_call(
        functools.partial(_mm_qknorm_kernel, tn=tn),
        grid=(M // tm, ncols // tn),
        in_specs=[pl.BlockSpec((tm, K), lambda i, j: (i, 0)),
                  pl.BlockSpec((K, tn), lambda i, j: (0, j)),
                  pl.BlockSpec((1, tn), lambda i, j: (0, j))],
        out_specs=pl.BlockSpec((tm, tn), lambda i, j: (i, j)),
        out_shape=jax.ShapeDtypeStruct((M, ncols), BF16),
        compiler_params=_cparams(("parallel", "arbitrary")),
        name="in_proj_qk",
    )(xn, w, g)


POS_SPLIT = 16


def _alibi_lanes(slopes, tile):
    p = jnp.stack(_split3_bf16(slopes), axis=1)
    r = jnp.arange(tile, dtype=jnp.int32)
    pa = (r & ~(POS_SPLIT - 1)).astype(F32)
    pb = (r & (POS_SPLIT - 1)).astype(F32)
    H = slopes.shape[0]
    pos = jnp.broadcast_to(jnp.stack([pa, pa, pa, pb, pb, pb], axis=1)[None], (H, tile, 6))
    pieces = jnp.broadcast_to(jnp.concatenate([p, p], axis=1)[:, None, :], (H, tile, 6))
    pad = jnp.zeros((H, tile, LANES - 12), F32)
    qaug = jnp.concatenate([pieces, pos, pad], axis=2)
    kaug = jnp.concatenate([pos, -pieces, pad], axis=2)
    return qaug.astype(BF16), jnp.stack([kaug, -kaug], axis=1).astype(BF16), jnp.sum(p, axis=1)


def _attn_kernel(lam_ref, sl_ref, dist_ref, q_ref, k_ref, v_ref, g_ref, qaug_ref, kaug_ref, absd_ref, o_ref,
                 acc_ref, m_ref, s_ref, rm_ref, *, tq, tk, nk, out_scale):
    h = pl.program_id(1)
    qi = pl.program_id(2)
    q0 = qi * tq
    slope = sl_ref[h]
    lam = lam_ref[0]
    q = q_ref[...]
    lane = lax.broadcasted_iota(jnp.int32, (1, LANES), 1)
    zero = jnp.zeros_like(q)
    qs = jnp.concatenate([jnp.where(lane < DQK, q, zero), jnp.where(lane >= DQK, q, zero)], axis=0)
    qaug = qaug_ref[0]
    qcat = jnp.concatenate([qs, jnp.concatenate([qaug, qaug], axis=0)], axis=1)
    ones_col = jnp.broadcast_to(jnp.where(lane == 0, 1.0, 0.0).astype(BF16), (tk, LANES))
    nt = (((1,), (1,)), ((), ()))

    m_ref[...] = jnp.full(m_ref.shape, -1e30, F32)
    acc_ref[...] = jnp.zeros(acc_ref.shape, F32)

    def row_max(sp):
        return jnp.broadcast_to(jnp.max(sp, axis=-1, keepdims=True), (2 * tq, LANES))

    def update(sp, rm, c, start):
        m_old = m_ref[...]
        m_new = jnp.maximum(m_old, rm + c)
        alpha = jnp.exp2(m_old - m_new)
        mu = m_new - c
        p = jnp.exp2(sp - jnp.concatenate([mu] * (tk // LANES), axis=1)).astype(BF16)
        vcat = jnp.concatenate([v_ref[pl.ds(start, tk), :], ones_col], axis=1)
        acc_ref[...] = (jnp.concatenate([alpha, alpha], axis=1) * acc_ref[...]
                        + jnp.dot(p, vcat, preferred_element_type=F32))
        m_ref[...] = m_new

    dist = dist_ref[h]
    lo = jnp.maximum(q0 - dist, 0) // tk
    hi = jnp.minimum((q0 + tq - 1 + dist) // tk + 1, nk)
    n_left = qi - lo
    n_side = n_left + (hi - qi - 1)

    def tile_of(t):
        t = jnp.minimum(t, n_side - 1)
        return jnp.clip(jnp.where(t < n_left, lo + t, qi + 1 + t - n_left), 0, nk - 1), t < n_left

    def scores(slot, t):
        ki, is_left = tile_of(t)
        start = pl.multiple_of(ki * tk, tk)
        kaug = kaug_ref[0, jnp.where(is_left, 0, 1)]
        kc = jnp.concatenate([k_ref[pl.ds(start, tk), :], kaug], axis=1)
        sp = lax.dot_general(qcat, kc, nt, preferred_element_type=F32)
        s_ref[slot] = sp
        rm_ref[slot] = row_max(sp)

    def consume(slot, t):
        ki, is_left = tile_of(t)
        dist0 = jnp.where(is_left, q0 - ki * tk, ki * tk - q0)
        update(s_ref[slot], rm_ref[slot], -slope * lax.convert_element_type(dist0, F32),
               pl.multiple_of(ki * tk, tk))

    def pair(t0):
        scores(1, t0 + 1)
        consume(0, t0)
        scores(0, t0 + 2)
        consume(1, t0 + 1)

    def quad(i, carry):
        pair(4 * i)
        pair(4 * i + 2)
        return carry

    scores(0, 0)
    start_d = pl.multiple_of(q0, tk)
    bias = absd_ref[...] * (-slope)
    sp_d = (lax.dot_general(qs, k_ref[pl.ds(start_d, tk), :], nt, preferred_element_type=F32)
            + jnp.concatenate([bias, bias], axis=0))
    update(sp_d, row_max(sp_d), 0.0, start_d)
    n_quads = n_side // 4
    lax.fori_loop(0, n_quads, quad, 0)
    rem = n_side - 4 * n_quads

    @pl.when(rem >= 2)
    def _():
        pair(4 * n_quads)

    @pl.when(rem % 2 == 1)
    def _():
        consume(0, n_side - 1)

    acc = acc_ref[...]
    o = acc[:tq, :DV] / acc[:tq, DV:DV + 1] - lam * (acc[tq:, :DV] / acc[tq:, DV:DV + 1])
    ms = jnp.mean(o * o, axis=-1, keepdims=True)
    o_ref[...] = (o * lax.rsqrt(ms + RMS_EPS) * (g_ref[...] * out_scale)).astype(o_ref.dtype)


def _split3_bf16(x):
    p0 = x.astype(BF16).astype(F32)
    p1 = (x - p0).astype(BF16).astype(F32)
    p2 = (x - p0 - p1).astype(BF16).astype(F32)
    return p0, p1, p2


F32_EXP_ZERO = 104.0


def _zero_weight_distance(gq, gk, slopes_nat, L):
    bound = 8.0 * jnp.max(jnp.abs(gq.astype(F32))) * jnp.max(jnp.abs(gk.astype(F32))) * 1.02
    d = jnp.ceil((F32_EXP_ZERO + 2.0 * bound) / slopes_nat) + 1.0
    return jnp.clip(d, 1.0, float(L)).astype(jnp.int32)


def _attention(qkn, v, lam, sub_g, gq, gk, lam_init, nb, L, tile=512):
    T = qkn.shape[0]
    tq = tk = min(tile, L)
    nq = L // tq
    slopes_nat = jnp.exp2(-8.0 * jnp.arange(1, N_HEADS + 1, dtype=F32) / N_HEADS)
    qaug, kaug, sl = _alibi_lanes(slopes_nat * LOG2E, tq)
    idx = jnp.arange(tq, dtype=jnp.int32)
    absd = jnp.abs(idx[:, None] - idx[None, :]).astype(F32)
    dist = _zero_weight_distance(gq, gk, slopes_nat, L)
    kern = functools.partial(_attn_kernel, tq=tq, tk=tk, nk=L // tk, out_scale=1.0 - lam_init)
    return pl.pallas_call(
        kern,
        grid=(nb, N_HEADS, nq),
        in_specs=[pl.BlockSpec(memory_space=pltpu.SMEM),
                  pl.BlockSpec(memory_space=pltpu.SMEM),
                  pl.BlockSpec(memory_space=pltpu.SMEM),
                  pl.BlockSpec((tq, LANES), lambda b, h, i: (b * nq + i, h)),
                  pl.BlockSpec((L, LANES), lambda b, h, i: (b, N_HEADS + h)),
                  pl.BlockSpec((L, LANES), lambda b, h, i: (b, h)),
                  pl.BlockSpec((1, DV), lambda b, h, i: (0, 0)),
                  pl.BlockSpec((1, tq, LANES), lambda b, h, i: (h, 0, 0)),
                  pl.BlockSpec((1, 2, tk, LANES), lambda b, h, i: (h, 0, 0, 0)),
                  pl.BlockSpec((tq, tk), lambda b, h, i: (0, 0))],
        out_specs=pl.BlockSpec((tq, LANES), lambda b, h, i: (b * nq + i, h)),
        out_shape=jax.ShapeDtypeStruct((T, N_HEADS * DV), BF16),
        scratch_shapes=[pltpu.VMEM((2 * tq, 2 * LANES), F32), pltpu.VMEM((2 * tq, LANES), F32),
                        pltpu.VMEM((2, 2 * tq, tk), F32), pltpu.VMEM((2, 2 * tq, LANES), F32)],
        compiler_params=_cparams(("parallel", "parallel", "arbitrary")),
        name="diff_attn",
    )(lam.reshape(1).astype(F32), sl, dist, qkn, qkn, v, sub_g.reshape(1, DV).astype(F32), qaug, kaug, absd)


def _s5_prep(A_re, A_im, log_dt, B_re, B_im, C_re, C_im, d_skip, nc):
    tc = S5_CHUNK
    hp = lax.Precision.HIGHEST
    A_re, A_im, log_dt = A_re.astype(F32), A_im.astype(F32), log_dt.astype(F32)
    B_re, B_im, C_re, C_im = B_re.astype(F32), B_im.astype(F32), C_re.astype(F32), C_im.astype(F32)
    G = A_re.shape[1]
    dt = jnp.exp(log_dt)[..., None]
    ang = A_im * dt
    mag = jnp.exp(A_re * dt)
    lr, li = mag * jnp.cos(ang), mag * jnp.sin(ang)
    den = A_re * A_re + A_im * A_im
    cr = ((lr - 1.0) * A_re + li * A_im) / den
    ci = (li * A_re - (lr - 1.0) * A_im) / den
    bbr = cr[..., None] * B_re - ci[..., None] * B_im
    bbi = cr[..., None] * B_im + ci[..., None] * B_re

    def lam_pow(n):
        nn = n.astype(F32)[:, None, None, None]
        pm = jnp.exp(A_re * dt * nn)
        pa = ang * nn
        return pm * jnp.cos(pa), pm * jnp.sin(pa)

    pr, pi = lam_pow(jnp.arange(tc + 1))
    lbr = pr[..., None] * bbr - pi[..., None] * bbi
    lbi = pr[..., None] * bbi + pi[..., None] * bbr
    kk = (jnp.einsum('dghp,ndgpb->ndghb', C_re, lbr, precision=hp)
          - jnp.einsum('dghp,ndgpb->ndghb', C_im, lbi, precision=hp))
    j = jnp.arange(tc)[:, None]
    t = jnp.arange(tc)[None, :]
    kf = jnp.where((t >= j)[:, :, None, None, None], kk[jnp.clip(t - j, 0, tc), 0], 0.0)
    kb = jnp.where((j >= t)[:, :, None, None, None], kk[jnp.clip(j - t, 0, tc), 1], 0.0)
    eye_t = (j == t).astype(F32)[:, :, None, None, None]
    eye_c = jnp.eye(SSM_CH, dtype=F32)[None, None, None]
    dd = eye_t * eye_c * d_skip.astype(F32)[None, None, :, :, None]
    ty = (kf + kb + dd).transpose(2, 0, 4, 1, 3).reshape(G, tc * SSM_CH, tc * SSM_CH)

    def state_cols(x):
        return x.transpose(1, 0, 3, 2).reshape(G, tc * SSM_CH, SSM_STATE)

    nf = tc - 1 - jnp.arange(tc)
    nbk = jnp.arange(tc)
    w1 = jnp.concatenate([ty, state_cols(lbr[nf, 0]), state_cols(lbi[nf, 0]),
                          state_cols(lbr[nbk, 1]), state_cols(lbi[nbk, 1])], axis=-1)

    def carry_rows(n, d):
        a = C_re[d][None] * pr[n, d][:, :, None, :] - C_im[d][None] * pi[n, d][:, :, None, :]
        bco = C_re[d][None] * pi[n, d][:, :, None, :] + C_im[d][None] * pr[n, d][:, :, None, :]
        to_rows = lambda x: x.transpose(1, 3, 0, 2).reshape(G, SSM_STATE, tc * SSM_CH)
        return to_rows(a), to_rows(-bco)

    fa, fb = carry_rows(jnp.arange(tc) + 1, 0)
    ba, bb = carry_rows(tc - jnp.arange(tc), 1)
    w2 = jnp.concatenate([fa, fb, ba, bb], axis=1)

    nsteps = max(1, int(math.ceil(math.log2(nc))))
    sr, si = lam_pow(tc * (2 ** jnp.arange(nsteps)))
    rows = jnp.stack([jnp.concatenate([sr, sr], -1), jnp.concatenate([-si, si], -1)], axis=2)
    coef = rows.transpose(3, 0, 1, 2, 4).reshape(G, nsteps * 4, LANES)
    return w1.astype(BF16), w2.astype(BF16), coef, nsteps


GROUPS_PER_TILE = LANES // SSM_CH


def _granule_transpose(arrs):
    lane = lax.broadcasted_iota(jnp.int32, (1, LANES), 1)
    a = list(arrs)
    for dist, shift in ((4, 4 * SSM_CH), (2, 2 * SSM_CH), (1, SSM_CH)):
        keep_lo = (lane & (2 * shift - 1)) < shift
        new = list(a)
        for i in range(GROUPS_PER_TILE):
            if (i // dist) % 2 == 0:
                lo, hi = a[i], a[i + dist]
                new[i] = jnp.where(keep_lo, lo, pltpu.roll(hi, shift, axis=1))
                new[i + dist] = jnp.where(keep_lo, pltpu.roll(lo, LANES - shift, axis=1), hi)
        a = new
    return a


def _s5_kernel(u_ref, w1_ref, w2_ref, c_ref, o_ref, ug_ref, yg_ref, *, nc, nsteps):
    ny = S5_CHUNK * SSM_CH
    halves = S5_CHUNK // GROUPS_PER_TILE
    for hf in range(halves):
        xs = [u_ref[pl.ds(GROUPS_PER_TILE * hf + j, nc, stride=S5_CHUNK), :] for j in range(GROUPS_PER_TILE)]
        for g, x in enumerate(_granule_transpose(xs)):
            ug_ref[g, :, hf * LANES:(hf + 1) * LANES] = x.astype(BF16)

    row = lax.broadcasted_iota(jnp.int32, (nc, 1), 0)

    def shifted(x, d, fwd):
        if d % SUBLANES == 0:
            pad = jnp.zeros((d, x.shape[1]), x.dtype)
            return jnp.concatenate([pad, x[:nc - d]] if fwd else [x[d:], pad], axis=0)
        if fwd:
            return jnp.where(row >= d, pltpu.roll(x, d, axis=0), 0.0)
        return jnp.where(row < nc - d, pltpu.roll(x, nc - d, axis=0), 0.0)

    def group(g, carry):
        r = jnp.dot(ug_ref[g], w1_ref[g], preferred_element_type=F32)

        def carry_in(x, dirn):
            s = shifted(x, 1, dirn == 0)
            for st in range(nsteps):
                d = 2 ** st
                if d >= nc:
                    break
                sh = shifted(s, d, dirn == 0)
                a = c_ref[g, 4 * st + 2 * dirn:4 * st + 2 * dirn + 1, :]
                b = c_ref[g, 4 * st + 2 * dirn + 1:4 * st + 2 * dirn + 2, :]
                s = s + a * sh + b * pltpu.roll(sh, SSM_STATE, axis=1)
            return s

        sf = carry_in(r[:, ny:ny + LANES], 0)
        sb = carry_in(r[:, ny + LANES:ny + 2 * LANES], 1)
        s = jnp.concatenate([sf, sb], axis=1).astype(BF16)
        yg_ref[g] = r[:, :ny] + jnp.dot(s, w2_ref[g], preferred_element_type=F32)
        return carry

    lax.fori_loop(0, GROUPS_PER_TILE, group, 0)

    for hf in range(halves):
        ys = [yg_ref[g, :, hf * LANES:(hf + 1) * LANES] for g in range(GROUPS_PER_TILE)]
        for j, x in enumerate(_granule_transpose(ys)):
            o_ref[pl.ds(GROUPS_PER_TILE * hf + j, nc, stride=S5_CHUNK), :] = x


def _s5(u, prep, nb, L):
    w1, w2, coef, nsteps = prep
    T, W = u.shape
    gt = GROUPS_PER_TILE
    nc = L // S5_CHUNK
    ny = S5_CHUNK * SSM_CH
    return pl.pallas_call(
        functools.partial(_s5_kernel, nc=nc, nsteps=nsteps),
        grid=(nb, W // LANES),
        in_specs=[pl.BlockSpec((L, LANES), lambda b, t: (b, t), pipeline_mode=pl.Buffered(1)),
                  pl.BlockSpec((gt, ny, 2 * ny), lambda b, t: (t, 0, 0)),
                  pl.BlockSpec((gt, ny, ny), lambda b, t: (t, 0, 0)),
                  pl.BlockSpec((gt, 4 * nsteps, LANES), lambda b, t: (t, 0, 0))],
        out_specs=pl.BlockSpec((L, LANES), lambda b, t: (b, t), pipeline_mode=pl.Buffered(1)),
        out_shape=jax.ShapeDtypeStruct((T, W), F32),
        scratch_shapes=[pltpu.VMEM((gt, nc, ny), BF16), pltpu.VMEM((gt, nc, ny), F32)],
        compiler_params=_cparams(("parallel", "parallel")),
        name="s5_chunked",
    )(u, w1, w2, coef)


def _glu_kernel(y_ref, w_ref, b_ref, g_ref, o_ref):
    y = y_ref[...]
    c0 = math.sqrt(2.0 / math.pi)
    gl = 0.5 * y * (1.0 + jnp.tanh(c0 * (y + 0.044715 * (y * y * y))))
    z = jnp.dot(gl.astype(BF16), w_ref[...], preferred_element_type=F32) + b_ref[...]
    s = gl * (1.0 / (1.0 + jnp.exp(-z)))
    ms = jnp.mean(s * s, axis=-1, keepdims=True)
    o_ref[...] = (s * lax.rsqrt(ms + RMS_EPS) * g_ref[...]).astype(o_ref.dtype)


def _glu_norm(y, w, b, g, tm=512):
    T, W = y.shape
    tm = min(tm, T)
    return pl.pallas_call(
        _glu_kernel,
        grid=(T // tm,),
        in_specs=[pl.BlockSpec((tm, W), lambda i: (i, 0)),
                  pl.BlockSpec((W, W), lambda i: (0, 0)),
                  pl.BlockSpec((1, W), lambda i: (0, 0)),
                  pl.BlockSpec((1, W), lambda i: (0, 0))],
        out_specs=pl.BlockSpec((tm, W), lambda i: (i, 0)),
        out_shape=jax.ShapeDtypeStruct((T, W), BF16),
        compiler_params=_cparams(("parallel",)),
        name="glu_norm",
    )(y, w, b.reshape(1, W).astype(F32), g.reshape(1, W).astype(F32))


def _out_proj_kernel(a_ref, s_ref, wa_ref, ws_ref, x_ref, o_ref):
    acc = jnp.dot(a_ref[...], wa_ref[...], preferred_element_type=F32)
    acc = acc + jnp.dot(s_ref[...], ws_ref[...], preferred_element_type=F32)
    o_ref[...] = x_ref[...] + acc


def _out_proj(att, ssm, w, x, tm=1024, tn=512):
    T, Wa = att.shape
    D = x.shape[1]
    tm = min(tm, T)
    tn = min(tn, D)
    rb = 1
    return pl.pallas_call(
        _out_proj_kernel,
        grid=(T // tm, D // tn),
        in_specs=[pl.BlockSpec((tm, Wa), lambda i, j: (i, 0)),
                  pl.BlockSpec((tm, Wa), lambda i, j: (i, 0)),
                  pl.BlockSpec((Wa, tn), lambda i, j: (0, j)),
                  pl.BlockSpec((Wa, tn), lambda i, j: (rb, j)),
                  pl.BlockSpec((tm, tn), lambda i, j: (i, j))],
        out_specs=pl.BlockSpec((tm, tn), lambda i, j: (i, j)),
        out_shape=jax.ShapeDtypeStruct((T, D), F32),
        compiler_params=_cparams(("parallel", "arbitrary")),
        name="out_proj",
    )(att, ssm, w, w, x)


def _router_kernel(h_ref, g_ref, w_ref, b_ref, xn_ref, lg_ref):
    x = h_ref[...]
    ms = jnp.mean(x * x, axis=-1, keepdims=True)
    xn = x * lax.rsqrt(ms + RMS_EPS) * g_ref[...]
    xn_ref[...] = _pack_halves(xn)
    lg_ref[...] = jnp.dot(xn, w_ref[...], preferred_element_type=F32,
                          precision=lax.Precision.HIGHEST) + b_ref[...]


def _norm_router(h, g, rw, rb, tm=256):
    T, D = h.shape
    tm = min(tm, T)
    return pl.pallas_call(
        _router_kernel,
        grid=(T // tm,),
        in_specs=[pl.BlockSpec((tm, D), lambda i: (i, 0)),
                  pl.BlockSpec((1, D), lambda i: (0, 0)),
                  pl.BlockSpec((D, ROUTER_PAD), lambda i: (0, 0)),
                  pl.BlockSpec((1, ROUTER_PAD), lambda i: (0, 0))],
        out_specs=[pl.BlockSpec((tm, D // 2), lambda i: (i, 0)),
                   pl.BlockSpec((tm, ROUTER_PAD), lambda i: (i, 0))],
        out_shape=[jax.ShapeDtypeStruct((T, D // 2), jnp.uint32), jax.ShapeDtypeStruct((T, ROUTER_PAD), F32)],
        compiler_params=_cparams(("parallel",)),
        name="norm_router",
    )(h, g.reshape(1, D).astype(F32), rw, rb)


def _pack_halves(x):
    half = x.shape[1] // 2
    lo = pltpu.bitcast(x[:, :half].astype(BF16).astype(F32), jnp.uint32)
    hi = pltpu.bitcast(x[:, half:].astype(BF16).astype(F32), jnp.uint32)
    return hi | (lo >> 16)


def _unpack_halves(w):
    lo = pltpu.bitcast(w << 16, F32)
    hi = pltpu.bitcast(w & jnp.uint32(0xFFFF0000), F32)
    return lo, hi


def _row_copy(src_hbm, src_row, dst, dst_row, sem):
    return pltpu.make_async_copy(src_hbm.at[pl.ds(src_row, 1), :], dst.at[pl.ds(dst_row, 1), :], sem)


def _gather_start(idx_ref, src_hbm, dst, sem, n):
    for r in range(n):
        _row_copy(src_hbm, idx_ref[0, 0, r], dst, r, sem).start()


def _gather_wait(src_hbm, dst, sem, n):
    for r in range(n):
        _row_copy(src_hbm, 0, dst, r, sem).wait()


def _expert_up_kernel(be_ref, nu_ref, tok_ref, tokn_ref, x_hbm, wg_ref, wu_ref, h_ref, xbuf, sem, *, blk):
    b = pl.program_id(0)
    nused = nu_ref[0]
    slot = b % 2

    @pl.when(b == 0)
    def _():
        _gather_start(tok_ref, x_hbm, xbuf.at[0], sem.at[0], blk)

    @pl.when(b < nused)
    def _():
        _gather_wait(x_hbm, xbuf.at[slot], sem.at[slot], blk)
        _gather_start(tokn_ref, x_hbm, xbuf.at[1 - slot], sem.at[1 - slot], blk)
        lo, hi = _unpack_halves(xbuf[slot])
        x = jnp.concatenate([lo, hi], axis=1).astype(BF16)
        g = jnp.dot(x, wg_ref[0], preferred_element_type=F32)
        u = jnp.dot(x, wu_ref[0], preferred_element_type=F32)
        h_ref[...] = (g * (1.0 / (1.0 + jnp.exp(-g))) * u).astype(h_ref.dtype)

    @pl.when(b >= nused)
    def _():
        h_ref[...] = jnp.zeros_like(h_ref)

    @pl.when(b == nused)
    def _():
        _gather_wait(x_hbm, xbuf.at[slot], sem.at[slot], blk)


def _expert_down_kernel(be_ref, nu_ref, h_ref, wd_ref, o_ref):
    live = pl.program_id(0) < nu_ref[0]

    @pl.when(live)
    def _():
        o_ref[...] = _pack_halves(jnp.dot(h_ref[...], wd_ref[0], preferred_element_type=F32))

    @pl.when(jnp.logical_not(live))
    def _():
        o_ref[...] = jnp.zeros_like(o_ref)


def _experts(xn2, blk_e, nused, slot_tok, wg, wu, wd, blk):
    T, Dh = xn2.shape
    D = 2 * Dh
    nblk = slot_tok.shape[0]
    dff = wg.shape[2]

    def wmap(b, be, nu):
        return (be[jnp.minimum(b, nu[0] - 1)], 0, 0)

    hid = pl.pallas_call(
        functools.partial(_expert_up_kernel, blk=blk),
        grid_spec=pltpu.PrefetchScalarGridSpec(
            num_scalar_prefetch=2,
            grid=(nblk,),
            in_specs=[pl.BlockSpec((1, 1, blk), lambda b, be, nu: (b, 0, 0), memory_space=pltpu.SMEM),
                      pl.BlockSpec((1, 1, blk), lambda b, be, nu: (jnp.minimum(b + 1, nu[0] - 1), 0, 0),
                                   memory_space=pltpu.SMEM),
                      pl.BlockSpec(memory_space=pl.ANY),
                      pl.BlockSpec((1, D, dff), wmap),
                      pl.BlockSpec((1, D, dff), wmap)],
            out_specs=pl.BlockSpec((blk, dff), lambda b, be, nu: (b, 0)),
            scratch_shapes=[pltpu.VMEM((2, blk, Dh), jnp.uint32), pltpu.SemaphoreType.DMA((2,))],
        ),
        out_shape=jax.ShapeDtypeStruct((nblk * blk, dff), BF16),
        compiler_params=_cparams(("arbitrary",)),
        name="experts_up",
    )(blk_e, nused, slot_tok, slot_tok, xn2, wg, wu)
    return pl.pallas_call(
        _expert_down_kernel,
        grid_spec=pltpu.PrefetchScalarGridSpec(
            num_scalar_prefetch=2,
            grid=(nblk,),
            in_specs=[pl.BlockSpec((blk, dff), lambda b, be, nu: (b, 0)),
                      pl.BlockSpec((1, dff, D), wmap)],
            out_specs=pl.BlockSpec((blk, Dh), lambda b, be, nu: (b, 0)),
        ),
        out_shape=jax.ShapeDtypeStruct((nblk * blk, Dh), jnp.uint32),
        compiler_params=_cparams(("arbitrary",)),
        name="experts_down",
    )(blk_e, nused, hid, wd)


def _combine_kernel(d_ref, dn_ref, h_ref, gw_ref, ys_hbm, o_ref, buf, sem, *, tm, nt):
    i = pl.program_id(0)
    slot = i % 2
    half = h_ref.shape[1] // 2

    @pl.when(i == 0)
    def _():
        _gather_start(d_ref, ys_hbm, buf.at[0], sem.at[0], 2 * tm)

    _gather_wait(ys_hbm, buf.at[slot], sem.at[slot], 2 * tm)
    _gather_start(dn_ref, ys_hbm, buf.at[1 - slot], sem.at[1 - slot], 2 * tm)
    lo0, hi0 = _unpack_halves(buf[slot, :tm, :])
    lo1, hi1 = _unpack_halves(buf[slot, tm:, :])
    w0 = gw_ref[:, 0:1]
    w1 = gw_ref[:, 1:2]
    o_ref[:, :half] = h_ref[:, :half] + w0 * lo0 + w1 * lo1
    o_ref[:, half:] = h_ref[:, half:] + w0 * hi0 + w1 * hi1

    @pl.when(i == nt - 1)
    def _():
        _gather_wait(ys_hbm, buf.at[1 - slot], sem.at[1 - slot], 2 * tm)


def _combine(h, ys, dest, gate, tm=128):
    T, D = h.shape
    tm = min(tm, T)
    nt = T // tm
    d3 = dest.reshape(nt, tm, 2).transpose(0, 2, 1).reshape(nt, 1, 2 * tm)
    return pl.pallas_call(
        functools.partial(_combine_kernel, tm=tm, nt=nt),
        grid=(nt,),
        in_specs=[pl.BlockSpec((1, 1, 2 * tm), lambda i: (i, 0, 0), memory_space=pltpu.SMEM),
                  pl.BlockSpec((1, 1, 2 * tm), lambda i: (jnp.minimum(i + 1, nt - 1), 0, 0),
                               memory_space=pltpu.SMEM),
                  pl.BlockSpec((tm, D), lambda i: (i, 0)),
                  pl.BlockSpec((tm, TOP_K_INNER), lambda i: (i, 0)),
                  pl.BlockSpec(memory_space=pl.ANY)],
        out_specs=pl.BlockSpec((tm, D), lambda i: (i, 0)),
        out_shape=jax.ShapeDtypeStruct((T, D), F32),
        scratch_shapes=[pltpu.VMEM((2, 2 * tm, D // 2), jnp.uint32), pltpu.SemaphoreType.DMA((2,))],
        compiler_params=_cparams(("arbitrary",)),
        name="moe_combine",
    )(d3, d3, h, gate, ys)


def _route(logits, blk):
    T = logits.shape[0]
    ng, epg, ne = N_EXPERT_GROUPS, EXPERTS_PER_GROUP, N_EXPERTS
    gp = jax.nn.softmax(logits[:, :ng], axis=-1)
    gsel = jnp.argmax(gp, axis=-1).astype(jnp.int32)
    gprob = jnp.take_along_axis(gp, gsel[:, None], axis=1)[:, 0]
    elog = logits[:, ng:ng + ne].reshape(T, ng, epg)
    elog = jnp.take_along_axis(elog, gsel[:, None, None], axis=1)[:, 0]
    topv, topi = lax.top_k(jax.nn.softmax(elog, axis=-1), TOP_K_INNER)
    gate = topv / jnp.sum(topv, axis=-1, keepdims=True) * gprob[:, None]
    eid = (gsel[:, None] * epg + topi.astype(jnp.int32)).reshape(-1)
    N = T * TOP_K_INNER
    piece = min(256, N)
    onehot = (eid[:, None] == jnp.arange(ne, dtype=jnp.int32)[None, :]).astype(F32).reshape(N // piece, piece, ne)
    hp = lax.Precision.HIGHEST
    within = jnp.einsum('ij,pjk->pik', jnp.tril(jnp.ones((piece, piece), F32)), onehot, precision=hp)
    totals = within[:, -1, :]
    before = jnp.einsum('pq,qk->pk', jnp.tril(jnp.ones((N // piece, N // piece), F32), -1), totals, precision=hp)
    csum = (within + before[:, None, :]).reshape(N, ne).astype(jnp.int32)
    rank = jnp.take_along_axis(csum, eid[:, None], axis=1)[:, 0] - 1
    counts = csum[-1]
    pcounts = ((counts + blk - 1) // blk) * blk
    pends = jnp.cumsum(pcounts)
    pstarts = pends - pcounts
    dest = pstarts[eid] + rank
    nblk = N // blk + ne
    tok = jnp.arange(N, dtype=jnp.int32) // TOP_K_INNER
    slot_tok = jnp.zeros((nblk * blk,), jnp.int32).at[dest].set(tok)
    blk_e = jnp.clip(jnp.searchsorted(pends, jnp.arange(nblk, dtype=jnp.int32) * blk, side='right'),
                     0, ne - 1).astype(jnp.int32)
    nused = (pends[-1] // blk).astype(jnp.int32).reshape(1)
    return (blk_e, nused, slot_tok.reshape(nblk, 1, blk), dest.reshape(T, TOP_K_INNER).astype(jnp.int32),
            gate.astype(F32))


def _moe(h, norm2_g, rw, rb, wg, wu, wd, blk=256):
    xn2, logits = _norm_router(h, norm2_g, rw, rb)
    blk = min(blk, h.shape[0])
    blk_e, nused, slot_tok, dest, gate = _route(logits, blk)
    ys = _experts(xn2, blk_e, nused, slot_tok, wg, wu, wd, blk)
    return _combine(h, ys, dest, gate)


def _layer(x3, lam_init, p):
    nb, L, D = x3.shape
    T = nb * L
    x = x3.reshape(T, D)
    w_in = p['w_in']
    att_w = N_HEADS * DV
    xn = _rmsnorm_cast(x, p['norm1_g'])
    qkn = _proj_qk(xn, w_in, p['q_norm_g'], p['k_norm_g'], 2 * att_w)
    v = _matmul(xn, w_in, 2 * att_w, att_w, BF16, "in_proj_v")
    u = _matmul(xn, w_in, 3 * att_w, w_in.shape[1] - 3 * att_w, F32, "in_proj_u")
    att = _attention(qkn, v, p['lam'], p['attn_sub_g'], p['q_norm_g'], p['k_norm_g'], lam_init, nb, L)
    y = _s5(u, p['s5'](L // S5_CHUNK), nb, L)
    ssm = _glu_norm(y, p['glu_w'], p['glu_b'], p['ssm_out_g'])
    h = _out_proj(att, ssm, p['w_out'], x)
    out = _moe(h, p['norm2_g'], p['rw'], p['rb'], p['wg'], p['wu'], p['wd'])
    return out.reshape(nb, L, D)


def kernel(x_prompt, x_sample, norm1_g, w_in, q_norm_g, k_norm_g, lam_q1, lam_k1, lam_q2, lam_k2, attn_sub_g, ssm_A_re, ssm_A_im, ssm_log_dt, ssm_B_re, ssm_B_im, ssm_C_re, ssm_C_im, ssm_D, glu_w, glu_b, ssm_out_g, w_out, norm2_g, router_group_w, router_group_b, router_expert_w, router_expert_b, exp_w_gate, exp_w_up, exp_w_down):
    depth = w_in.shape[0]

    def layer_params(l):
        lam_init = 0.8 - 0.6 * math.exp(-0.3 * l)
        lam = (jnp.exp(jnp.sum(lam_q1[l].astype(F32) * lam_k1[l].astype(F32)))
               - jnp.exp(jnp.sum(lam_q2[l].astype(F32) * lam_k2[l].astype(F32))) + lam_init)
        D = w_in.shape[1]
        ng, ne = router_group_w.shape[2], router_expert_w.shape[2]
        rw = jnp.zeros((D, ROUTER_PAD), F32)
        rw = rw.at[:, :ng].set(router_group_w[l].astype(F32)).at[:, ng:ng + ne].set(router_expert_w[l].astype(F32))
        rb = jnp.zeros((1, ROUTER_PAD), F32)
        rb = rb.at[0, :ng].set(router_group_b[l].astype(F32)).at[0, ng:ng + ne].set(router_expert_b[l].astype(F32))
        s5_cache = {}

        def s5(nc):
            if nc not in s5_cache:
                s5_cache[nc] = _s5_prep(ssm_A_re[l], ssm_A_im[l], ssm_log_dt[l], ssm_B_re[l], ssm_B_im[l],
                                        ssm_C_re[l], ssm_C_im[l], ssm_D[l], nc)
            return s5_cache[nc]

        return lam_init, dict(
            norm1_g=norm1_g[l], w_in=w_in[l].astype(BF16), q_norm_g=q_norm_g[l], k_norm_g=k_norm_g[l], lam=lam,
            attn_sub_g=attn_sub_g[l], s5=s5, glu_w=glu_w[l].astype(BF16), glu_b=glu_b[l], ssm_out_g=ssm_out_g[l],
            w_out=w_out[l].astype(BF16), norm2_g=norm2_g[l], rw=rw, rb=rb,
            wg=exp_w_gate[l].astype(BF16), wu=exp_w_up[l].astype(BF16), wd=exp_w_down[l].astype(BF16))

    params = [layer_params(l) for l in range(depth)]

    def run(x):
        for lam_init, p in params:
            x = _layer(x, lam_init, p)
        return x

    return (run(x_prompt), run(x_sample))
```

```python
import functools
import math

import jax
import jax.numpy as jnp
from jax import lax
from jax.experimental import pallas as pl
from jax.experimental.pallas import tpu as pltpu

F32 = jnp.float32
BF16 = jnp.bfloat16

RMS_EPS = 1e-6
LOG2E = 1.4426950408889634
LANES = 128
SUBLANES = 8
VMEM_LIMIT = 56 * 1024 * 1024

N_HEADS = 16
DQK = 64
DV = 2 * DQK
SSM_CH = 16
SSM_STATE = 64
S5_CHUNK = 16
N_EXPERT_GROUPS = 4
EXPERTS_PER_GROUP = 8
N_EXPERTS = N_EXPERT_GROUPS * EXPERTS_PER_GROUP
TOP_K_INNER = 2
ROUTER_PAD = LANES


def _cparams(sem):
    return pltpu.CompilerParams(dimension_semantics=sem, vmem_limit_bytes=VMEM_LIMIT)


def _rmsnorm_kernel(x_ref, g_ref, o_ref):
    x = x_ref[...]
    ms = jnp.mean(x * x, axis=-1, keepdims=True)
    o_ref[...] = (x * lax.rsqrt(ms + RMS_EPS) * g_ref[...]).astype(o_ref.dtype)


def _rmsnorm_cast(x, g, tm=512):
    T, D = x.shape
    tm = min(tm, T)
    return pl.pallas_call(
        _rmsnorm_kernel,
        grid=(T // tm,),
        in_specs=[pl.BlockSpec((tm, D), lambda i: (i, 0)),
                  pl.BlockSpec((1, D), lambda i: (0, 0))],
        out_specs=pl.BlockSpec((tm, D), lambda i: (i, 0)),
        out_shape=jax.ShapeDtypeStruct((T, D), BF16),
        compiler_params=_cparams(("parallel",)),
        name="rmsnorm1",
    )(x, g.reshape(1, D).astype(F32))


def _mm_kernel(a_ref, b_ref, o_ref):
    o_ref[...] = jnp.dot(a_ref[...], b_ref[...], preferred_element_type=F32).astype(o_ref.dtype)


def _matmul(a, b, col0, ncols, out_dtype, name, tm=1024, tn=512):
    M, K = a.shape
    tm = min(tm, M)
    tn = min(tn, ncols)
    cb = col0 // tn
    return pl.pallas_call(
        _mm_kernel,
        grid=(M // tm, ncols // tn),
        in_specs=[pl.BlockSpec((tm, K), lambda i, j: (i, 0)),
                  pl.BlockSpec((K, tn), lambda i, j: (0, j + cb))],
        out_specs=pl.BlockSpec((tm, tn), lambda i, j: (i, j)),
        out_shape=jax.ShapeDtypeStruct((M, ncols), out_dtype),
        compiler_params=_cparams(("parallel", "arbitrary")),
        name=name,
    )(a, b)


def _mm_qknorm_kernel(a_ref, b_ref, g_ref, o_ref, *, tn):
    acc = jnp.dot(a_ref[...], b_ref[...], preferred_element_type=F32)
    lane = lax.broadcasted_iota(jnp.int32, (1, LANES), 1)
    left = lane < DQK
    for c in range(tn // LANES):
        x = acc[:, c * LANES:(c + 1) * LANES]
        x2 = x * x
        s_all = jnp.sum(x2, axis=-1, keepdims=True)
        s_left = jnp.sum(jnp.where(left, x2, 0.0), axis=-1, keepdims=True)
        ms = jnp.where(left, s_left, s_all - s_left) * (1.0 / DQK)
        o_ref[:, c * LANES:(c + 1) * LANES] = (
            x * lax.rsqrt(ms + RMS_EPS) * g_ref[:, c * LANES:(c + 1) * LANES]).astype(o_ref.dtype)


def _proj_qk(xn, w, gq, gk, ncols, tm=1024, tn=512):
    M, K = xn.shape
    tm = min(tm, M)
    reps = ncols // (2 * DQK)
    g = jnp.concatenate([jnp.tile(gq.astype(F32) * (DQK ** -0.5 * LOG2E), reps),
                         jnp.tile(gk.astype(F32), reps)]).reshape(1, ncols)
    return pl.pallas_call(
        functools.partial(_mm_qknorm_kernel, tn=tn),
        grid=(M // tm, ncols // tn),
        in_specs=[pl.BlockSpec((tm, K), lambda i, j: (i, 0)),
                  pl.BlockSpec((K, tn), lambda i, j: (0, j)),
                  pl.BlockSpec((1, tn), lambda i, j: (0, j))],
        out_specs=pl.BlockSpec((tm, tn), lambda i, j: (i, j)),
        out_shape=jax.ShapeDtypeStruct((M, ncols), BF16),
        compiler_params=_cparams(("parallel", "arbitrary")),
        name="in_proj_qk",
    )(xn, w, g)


POS_SPLIT = 16


def _alibi_lanes(slopes, tile):
    p = jnp.stack(_split3_bf16(slopes), axis=1)
    r = jnp.arange(tile, dtype=jnp.int32)
    pa = (r & ~(POS_SPLIT - 1)).astype(F32)
    pb = (r & (POS_SPLIT - 1)).astype(F32)
    H = slopes.shape[0]
    pos = jnp.broadcast_to(jnp.stack([pa, pa, pa, pb, pb, pb], axis=1)[None], (H, tile, 6))
    pieces = jnp.broadcast_to(jnp.concatenate([p, p], axis=1)[:, None, :], (H, tile, 6))
    pad = jnp.zeros((H, tile, LANES - 12), F32)
    qaug = jnp.concatenate([pieces, pos, pad], axis=2)
    kaug = jnp.concatenate([pos, -pieces, pad], axis=2)
    return qaug.astype(BF16), jnp.stack([kaug, -kaug], axis=1).astype(BF16), jnp.sum(p, axis=1)


def _attn_kernel(lam_ref, sl_ref, dist_ref, q_ref, k_ref, v_ref, g_ref, qaug_ref, kaug_ref, absd_ref, o_ref,
                 acc_ref, m_ref, s_ref, rm_ref, *, tq, tk, nk, out_scale):
    h = pl.program_id(1)
    qi = pl.program_id(2)
    q0 = qi * tq
    slope = sl_ref[h]
    lam = lam_ref[0]
    q = q_ref[...]
    lane = lax.broadcasted_iota(jnp.int32, (1, LANES), 1)
    zero = jnp.zeros_like(q)
    qs = jnp.concatenate([jnp.where(lane < DQK, q, zero), jnp.where(lane >= DQK, q, zero)], axis=0)
    qaug = qaug_ref[0]
    qcat = jnp.concatenate([qs, jnp.concatenate([qaug, qaug], axis=0)], axis=1)
    ones_col = jnp.broadcast_to(jnp.where(lane == 0, 1.0, 0.0).astype(BF16), (tk, LANES))
    nt = (((1,), (1,)), ((), ()))

    m_ref[...] = jnp.full(m_ref.shape, -1e30, F32)
    acc_ref[...] = jnp.zeros(acc_ref.shape, F32)

    def row_max(sp):
        return jnp.broadcast_to(jnp.max(sp, axis=-1, keepdims=True), (2 * tq, LANES))

    def update(sp, rm, c, start):
        m_old = m_ref[...]
        m_new = jnp.maximum(m_old, rm + c)
        alpha = jnp.exp2(m_old - m_new)
        mu = m_new - c
        p = jnp.exp2(sp - jnp.concatenate([mu] * (tk // LANES), axis=1)).astype(BF16)
        vcat = jnp.concatenate([v_ref[pl.ds(start, tk), :], ones_col], axis=1)
        acc_ref[...] = (jnp.concatenate([alpha, alpha], axis=1) * acc_ref[...]
                        + jnp.dot(p, vcat, preferred_element_type=F32))
        m_ref[...] = m_new

    dist = dist_ref[h]
    lo = jnp.maximum(q0 - dist, 0) // tk
    hi = jnp.minimum((q0 + tq - 1 + dist) // tk + 1, nk)
    n_left = qi - lo
    n_side = n_left + (hi - qi - 1)

    def tile_of(t):
        t = jnp.minimum(t, n_side - 1)
        return jnp.clip(jnp.where(t < n_left, lo + t, qi + 1 + t - n_left), 0, nk - 1), t < n_left

    def scores(slot, t):
        ki, is_left = tile_of(t)
        start = pl.multiple_of(ki * tk, tk)
        kaug = kaug_ref[0, jnp.where(is_left, 0, 1)]
        kc = jnp.concatenate([k_ref[pl.ds(start, tk), :], kaug], axis=1)
        sp = lax.dot_general(qcat, kc, nt, preferred_element_type=F32)
        s_ref[slot] = sp
        rm_ref[slot] = row_max(sp)

    def consume(slot, t):
        ki, is_left = tile_of(t)
        dist0 = jnp.where(is_left, q0 - ki * tk, ki * tk - q0)
        update(s_ref[slot], rm_ref[slot], -slope * lax.convert_element_type(dist0, F32),
               pl.multiple_of(ki * tk, tk))

    def pair(t0):
        scores(1, t0 + 1)
        consume(0, t0)
        scores(0, t0 + 2)
        consume(1, t0 + 1)

    def quad(i, carry):
        pair(4 * i)
        pair(4 * i + 2)
        return carry

    scores(0, 0)
    start_d = pl.multiple_of(q0, tk)
    bias = absd_ref[...] * (-slope)
    sp_d = (lax.dot_general(qs, k_ref[pl.ds(start_d, tk), :], nt, preferred_element_type=F32)
            + jnp.concatenate([bias, bias], axis=0))
    update(sp_d, row_max(sp_d), 0.0, start_d)
    n_quads = n_side // 4
    lax.fori_loop(0, n_quads, quad, 0)
    rem = n_side - 4 * n_quads

    @pl.when(rem >= 2)
    def _():
        pair(4 * n_quads)

    @pl.when(rem % 2 == 1)
    def _():
        consume(0, n_side - 1)

    acc = acc_ref[...]
    o = acc[:tq, :DV] / acc[:tq, DV:DV + 1] - lam * (acc[tq:, :DV] / acc[tq:, DV:DV + 1])
    ms = jnp.mean(o * o, axis=-1, keepdims=True)
    o_ref[...] = (o * lax.rsqrt(ms + RMS_EPS) * (g_ref[...] * out_scale)).astype(o_ref.dtype)


def _split3_bf16(x):
    p0 = x.astype(BF16).astype(F32)
    p1 = (x - p0).astype(BF16).astype(F32)
    p2 = (x - p0 - p1).astype(BF16).astype(F32)
    return p0, p1, p2


F32_EXP_ZERO = 104.0


def _zero_weight_distance(gq, gk, slopes_nat, L):
    bound = 8.0 * jnp.max(jnp.abs(gq.astype(F32))) * jnp.max(jnp.abs(gk.astype(F32))) * 1.02
    d = jnp.ceil((F32_EXP_ZERO + 2.0 * bound) / slopes_nat) + 1.0
    return jnp.clip(d, 1.0, float(L)).astype(jnp.int32)


def _attention(qkn, v, lam, sub_g, gq, gk, lam_init, nb, L, tile=512):
    T = qkn.shape[0]
    tq = tk = min(tile, L)
    nq = L // tq
    slopes_nat = jnp.exp2(-8.0 * jnp.arange(1, N_HEADS + 1, dtype=F32) / N_HEADS)
    qaug, kaug, sl = _alibi_lanes(slopes_nat * LOG2E, tq)
    idx = jnp.arange(tq, dtype=jnp.int32)
    absd = jnp.abs(idx[:, None] - idx[None, :]).astype(F32)
    dist = _zero_weight_distance(gq, gk, slopes_nat, L)
    kern = functools.partial(_attn_kernel, tq=tq, tk=tk, nk=L // tk, out_scale=1.0 - lam_init)
    return pl.pallas_call(
        kern,
        grid=(nb, N_HEADS, nq),
        in_specs=[pl.BlockSpec(memory_space=pltpu.SMEM),
                  pl.BlockSpec(memory_space=pltpu.SMEM),
                  pl.BlockSpec(memory_space=pltpu.SMEM),
                  pl.BlockSpec((tq, LANES), lambda b, h, i: (b * nq + i, h)),
                  pl.BlockSpec((L, LANES), lambda b, h, i: (b, N_HEADS + h)),
                  pl.BlockSpec((L, LANES), lambda b, h, i: (b, h)),
                  pl.BlockSpec((1, DV), lambda b, h, i: (0, 0)),
                  pl.BlockSpec((1, tq, LANES), lambda b, h, i: (h, 0, 0)),
                  pl.BlockSpec((1, 2, tk, LANES), lambda b, h, i: (h, 0, 0, 0)),
                  pl.BlockSpec((tq, tk), lambda b, h, i: (0, 0))],
        out_specs=pl.BlockSpec((tq, LANES), lambda b, h, i: (b * nq + i, h)),
        out_shape=jax.ShapeDtypeStruct((T, N_HEADS * DV), BF16),
        scratch_shapes=[pltpu.VMEM((2 * tq, 2 * LANES), F32), pltpu.VMEM((2 * tq, LANES), F32),
                        pltpu.VMEM((2, 2 * tq, tk), F32), pltpu.VMEM((2, 2 * tq, LANES), F32)],
        compiler_params=_cparams(("parallel", "parallel", "arbitrary")),
        name="diff_attn",
    )(lam.reshape(1).astype(F32), sl, dist, qkn, qkn, v, sub_g.reshape(1, DV).astype(F32), qaug, kaug, absd)


def _s5_prep(A_re, A_im, log_dt, B_re, B_im, C_re, C_im, d_skip, nc):
    tc = S5_CHUNK
    hp = lax.Precision.HIGHEST
    A_re, A_im, log_dt = A_re.astype(F32), A_im.astype(F32), log_dt.astype(F32)
    B_re, B_im, C_re, C_im = B_re.astype(F32), B_im.astype(F32), C_re.astype(F32), C_im.astype(F32)
    G = A_re.shape[1]
    dt = jnp.exp(log_dt)[..., None]
    ang = A_im * dt
    mag = jnp.exp(A_re * dt)
    lr, li = mag * jnp.cos(ang), mag * jnp.sin(ang)
    den = A_re * A_re + A_im * A_im
    cr = ((lr - 1.0) * A_re + li * A_im) / den
    ci = (li * A_re - (lr - 1.0) * A_im) / den
    bbr = cr[..., None] * B_re - ci[..., None] * B_im
    bbi = cr[..., None] * B_im + ci[..., None] * B_re

    def lam_pow(n):
        nn = n.astype(F32)[:, None, None, None]
        pm = jnp.exp(A_re * dt * nn)
        pa = ang * nn
        return pm * jnp.cos(pa), pm * jnp.sin(pa)

    pr, pi = lam_pow(jnp.arange(tc + 1))
    lbr = pr[..., None] * bbr - pi[..., None] * bbi
    lbi = pr[..., None] * bbi + pi[..., None] * bbr
    kk = (jnp.einsum('dghp,ndgpb->dgbnh', C_re, lbr, precision=hp)
          - jnp.einsum('dghp,ndgpb->dgbnh', C_im, lbi, precision=hp))
    centre = (kk[0, :, :, 0, :] + kk[1, :, :, 0, :]
              + jnp.eye(SSM_CH, dtype=F32)[None] * d_skip.astype(F32)[:, None, :])
    strip = jnp.concatenate([kk[1, :, :, tc - 1:0:-1, :], centre[:, :, None, :], kk[0, :, :, 1:tc, :]], axis=2)
    strip = strip.reshape(G, SSM_CH, (2 * tc - 1) * SSM_CH)
    ty = jnp.stack([strip[:, :, (tc - 1 - j) * SSM_CH:(2 * tc - 1 - j) * SSM_CH] for j in range(tc)], axis=1)
    ty = ty.reshape(G, tc * SSM_CH, tc * SSM_CH)

    def state_cols(x):
        return x.transpose(1, 0, 3, 2).reshape(G, tc * SSM_CH, SSM_STATE)

    nf = tc - 1 - jnp.arange(tc)
    nbk = jnp.arange(tc)
    w1 = jnp.concatenate([ty, state_cols(lbr[nf, 0]), state_cols(lbi[nf, 0]),
                          state_cols(lbr[nbk, 1]), state_cols(lbi[nbk, 1])], axis=-1)

    def carry_rows(n, d):
        a = C_re[d][None] * pr[n, d][:, :, None, :] - C_im[d][None] * pi[n, d][:, :, None, :]
        bco = C_re[d][None] * pi[n, d][:, :, None, :] + C_im[d][None] * pr[n, d][:, :, None, :]
        to_rows = lambda x: x.transpose(1, 3, 0, 2).reshape(G, SSM_STATE, tc * SSM_CH)
        return to_rows(a), to_rows(-bco)

    fa, fb = carry_rows(jnp.arange(tc) + 1, 0)
    ba, bb = carry_rows(tc - jnp.arange(tc), 1)
    w2 = jnp.concatenate([fa, fb, ba, bb], axis=1)

    nsteps = max(1, int(math.ceil(math.log2(nc))))
    sr, si = lam_pow(tc * (2 ** jnp.arange(nsteps)))
    rows = jnp.stack([jnp.concatenate([sr, sr], -1), jnp.concatenate([-si, si], -1)], axis=2)
    coef = rows.transpose(3, 0, 1, 2, 4).reshape(G, nsteps * 4, LANES)
    return w1.astype(BF16), w2.astype(BF16), coef, nsteps


GROUPS_PER_TILE = LANES // SSM_CH


def _granule_transpose(arrs):
    lane = lax.broadcasted_iota(jnp.int32, (1, LANES), 1)
    a = list(arrs)
    for dist, shift in ((4, 4 * SSM_CH), (2, 2 * SSM_CH), (1, SSM_CH)):
        keep_lo = (lane & (2 * shift - 1)) < shift
        new = list(a)
        for i in range(GROUPS_PER_TILE):
            if (i // dist) % 2 == 0:
                lo, hi = a[i], a[i + dist]
                new[i] = jnp.where(keep_lo, lo, pltpu.roll(hi, shift, axis=1))
                new[i + dist] = jnp.where(keep_lo, pltpu.roll(lo, LANES - shift, axis=1), hi)
        a = new
    return a


def _s5_kernel(u_ref, w1_ref, w2_ref, c_ref, o_ref, ug_ref, yg_ref, *, nc, nsteps):
    ny = S5_CHUNK * SSM_CH
    halves = S5_CHUNK // GROUPS_PER_TILE
    for hf in range(halves):
        xs = [u_ref[pl.ds(GROUPS_PER_TILE * hf + j, nc, stride=S5_CHUNK), :] for j in range(GROUPS_PER_TILE)]
        for g, x in enumerate(_granule_transpose(xs)):
            ug_ref[g, :, hf * LANES:(hf + 1) * LANES] = x.astype(BF16)

    row = lax.broadcasted_iota(jnp.int32, (nc, 1), 0)

    def shifted(x, d, fwd):
        if d % SUBLANES == 0:
            pad = jnp.zeros((d, x.shape[1]), x.dtype)
            return jnp.concatenate([pad, x[:nc - d]] if fwd else [x[d:], pad], axis=0)
        if fwd:
            return jnp.where(row >= d, pltpu.roll(x, d, axis=0), 0.0)
        return jnp.where(row < nc - d, pltpu.roll(x, nc - d, axis=0), 0.0)

    def group(g, carry):
        r = jnp.dot(ug_ref[g], w1_ref[g], preferred_element_type=F32)

        def carry_in(x, dirn):
            s = shifted(x, 1, dirn == 0)
            for st in range(nsteps):
                d = 2 ** st
                if d >= nc:
                    break
                sh = shifted(s, d, dirn == 0)
                a = c_ref[g, 4 * st + 2 * dirn:4 * st + 2 * dirn + 1, :]
                b = c_ref[g, 4 * st + 2 * dirn + 1:4 * st + 2 * dirn + 2, :]
                s = s + a * sh + b * pltpu.roll(sh, SSM_STATE, axis=1)
            return s

        sf = carry_in(r[:, ny:ny + LANES], 0)
        sb = carry_in(r[:, ny + LANES:ny + 2 * LANES], 1)
        s = jnp.concatenate([sf, sb], axis=1).astype(BF16)
        yg_ref[g] = r[:, :ny] + jnp.dot(s, w2_ref[g], preferred_element_type=F32)
        return carry

    lax.fori_loop(0, GROUPS_PER_TILE, group, 0)

    for hf in range(halves):
        ys = [yg_ref[g, :, hf * LANES:(hf + 1) * LANES] for g in range(GROUPS_PER_TILE)]
        for j, x in enumerate(_granule_transpose(ys)):
            o_ref[pl.ds(GROUPS_PER_TILE * hf + j, nc, stride=S5_CHUNK), :] = x


def _s5(u, prep, nb, L):
    w1, w2, coef, nsteps = prep
    T, W = u.shape
    gt = GROUPS_PER_TILE
    nc = L // S5_CHUNK
    ny = S5_CHUNK * SSM_CH
    return pl.pallas_call(
        functools.partial(_s5_kernel, nc=nc, nsteps=nsteps),
        grid=(nb, W // LANES),
        in_specs=[pl.BlockSpec((L, LANES), lambda b, t: (b, t), pipeline_mode=pl.Buffered(1)),
                  pl.BlockSpec((gt, ny, 2 * ny), lambda b, t: (t, 0, 0)),
                  pl.BlockSpec((gt, ny, ny), lambda b, t: (t, 0, 0)),
                  pl.BlockSpec((gt, 4 * nsteps, LANES), lambda b, t: (t, 0, 0))],
        out_specs=pl.BlockSpec((L, LANES), lambda b, t: (b, t), pipeline_mode=pl.Buffered(1)),
        out_shape=jax.ShapeDtypeStruct((T, W), F32),
        scratch_shapes=[pltpu.VMEM((gt, nc, ny), BF16), pltpu.VMEM((gt, nc, ny), F32)],
        compiler_params=_cparams(("parallel", "parallel")),
        name="s5_chunked",
    )(u, w1, w2, coef)


def _glu_kernel(y_ref, w_ref, b_ref, g_ref, o_ref):
    y = y_ref[...]
    c0 = math.sqrt(2.0 / math.pi)
    gl = 0.5 * y * (1.0 + jnp.tanh(c0 * (y + 0.044715 * (y * y * y))))
    z = jnp.dot(gl.astype(BF16), w_ref[...], preferred_element_type=F32) + b_ref[...]
    s = gl * (1.0 / (1.0 + jnp.exp(-z)))
    ms = jnp.mean(s * s, axis=-1, keepdims=True)
    o_ref[...] = (s * lax.rsqrt(ms + RMS_EPS) * g_ref[...]).astype(o_ref.dtype)


def _glu_norm(y, w, b, g, tm=512):
    T, W = y.shape
    tm = min(tm, T)
    return pl.pallas_call(
        _glu_kernel,
        grid=(T // tm,),
        in_specs=[pl.BlockSpec((tm, W), lambda i: (i, 0)),
                  pl.BlockSpec((W, W), lambda i: (0, 0)),
                  pl.BlockSpec((1, W), lambda i: (0, 0)),
                  pl.BlockSpec((1, W), lambda i: (0, 0))],
        out_specs=pl.BlockSpec((tm, W), lambda i: (i, 0)),
        out_shape=jax.ShapeDtypeStruct((T, W), BF16),
        compiler_params=_cparams(("parallel",)),
        name="glu_norm",
    )(y, w, b.reshape(1, W).astype(F32), g.reshape(1, W).astype(F32))


def _out_proj_kernel(a_ref, s_ref, wa_ref, ws_ref, x_ref, o_ref):
    acc = jnp.dot(a_ref[...], wa_ref[...], preferred_element_type=F32)
    acc = acc + jnp.dot(s_ref[...], ws_ref[...], preferred_element_type=F32)
    o_ref[...] = x_ref[...] + acc


def _out_proj(att, ssm, w, x, tm=1024, tn=512):
    T, Wa = att.shape
    D = x.shape[1]
    tm = min(tm, T)
    tn = min(tn, D)
    rb = 1
    return pl.pallas_call(
        _out_proj_kernel,
        grid=(T // tm, D // tn),
        in_specs=[pl.BlockSpec((tm, Wa), lambda i, j: (i, 0)),
                  pl.BlockSpec((tm, Wa), lambda i, j: (i, 0)),
                  pl.BlockSpec((Wa, tn), lambda i, j: (0, j)),
                  pl.BlockSpec((Wa, tn), lambda i, j: (rb, j)),
                  pl.BlockSpec((tm, tn), lambda i, j: (i, j))],
        out_specs=pl.BlockSpec((tm, tn), lambda i, j: (i, j)),
        out_shape=jax.ShapeDtypeStruct((T, D), F32),
        compiler_params=_cparams(("parallel", "arbitrary")),
        name="out_proj",
    )(att, ssm, w, w, x)


def _router_kernel(h_ref, g_ref, w_ref, b_ref, xn_ref, lg_ref):
    x = h_ref[...]
    ms = jnp.mean(x * x, axis=-1, keepdims=True)
    xn = x * lax.rsqrt(ms + RMS_EPS) * g_ref[...]
    xn_ref[...] = _pack_halves(xn)
    lg_ref[...] = jnp.dot(xn, w_ref[...], preferred_element_type=F32,
                          precision=lax.Precision.HIGHEST) + b_ref[...]


def _norm_router(h, g, rw, rb, tm=256):
    T, D = h.shape
    tm = min(tm, T)
    return pl.pallas_call(
        _router_kernel,
        grid=(T // tm,),
        in_specs=[pl.BlockSpec((tm, D), lambda i: (i, 0)),
                  pl.BlockSpec((1, D), lambda i: (0, 0)),
                  pl.BlockSpec((D, ROUTER_PAD), lambda i: (0, 0)),
                  pl.BlockSpec((1, ROUTER_PAD), lambda i: (0, 0))],
        out_specs=[pl.BlockSpec((tm, D // 2), lambda i: (i, 0)),
                   pl.BlockSpec((tm, ROUTER_PAD), lambda i: (i, 0))],
        out_shape=[jax.ShapeDtypeStruct((T, D // 2), jnp.uint32), jax.ShapeDtypeStruct((T, ROUTER_PAD), F32)],
        compiler_params=_cparams(("parallel",)),
        name="norm_router",
    )(h, g.reshape(1, D).astype(F32), rw, rb)


def _pack_halves(x):
    half = x.shape[1] // 2
    lo = pltpu.bitcast(x[:, :half].astype(BF16).astype(F32), jnp.uint32)
    hi = pltpu.bitcast(x[:, half:].astype(BF16).astype(F32), jnp.uint32)
    return hi | (lo >> 16)


def _unpack_halves(w):
    lo = pltpu.bitcast(w << 16, F32)
    hi = pltpu.bitcast(w & jnp.uint32(0xFFFF0000), F32)
    return lo, hi


def _row_copy(src_hbm, src_row, dst, dst_row, sem):
    return pltpu.make_async_copy(src_hbm.at[pl.ds(src_row, 1), :], dst.at[pl.ds(dst_row, 1), :], sem)


def _gather_start(idx_ref, src_hbm, dst, sem, n):
    for r in range(n):
        _row_copy(src_hbm, idx_ref[0, 0, r], dst, r, sem).start()


def _gather_wait(src_hbm, dst, sem, n):
    for r in range(n):
        _row_copy(src_hbm, 0, dst, r, sem).wait()


def _expert_up_kernel(be_ref, nu_ref, tok_ref, tokn_ref, x_hbm, wg_ref, wu_ref, h_ref, xbuf, sem, *, blk):
    b = pl.program_id(0)
    nused = nu_ref[0]
    slot = b % 2

    @pl.when(b == 0)
    def _():
        _gather_start(tok_ref, x_hbm, xbuf.at[0], sem.at[0], blk)

    @pl.when(b < nused)
    def _():
        _gather_wait(x_hbm, xbuf.at[slot], sem.at[slot], blk)
        _gather_start(tokn_ref, x_hbm, xbuf.at[1 - slot], sem.at[1 - slot], blk)
        lo, hi = _unpack_halves(xbuf[slot])
        x = jnp.concatenate([lo, hi], axis=1).astype(BF16)
        g = jnp.dot(x, wg_ref[0], preferred_element_type=F32)
        u = jnp.dot(x, wu_ref[0], preferred_element_type=F32)
        h_ref[...] = (g * (1.0 / (1.0 + jnp.exp(-g))) * u).astype(h_ref.dtype)

    @pl.when(b >= nused)
    def _():
        h_ref[...] = jnp.zeros_like(h_ref)

    @pl.when(b == nused)
    def _():
        _gather_wait(x_hbm, xbuf.at[slot], sem.at[slot], blk)


def _expert_down_kernel(be_ref, nu_ref, h_ref, wd_ref, o_ref):
    live = pl.program_id(0) < nu_ref[0]

    @pl.when(live)
    def _():
        o_ref[...] = _pack_halves(jnp.dot(h_ref[...], wd_ref[0], preferred_element_type=F32))

    @pl.when(jnp.logical_not(live))
    def _():
        o_ref[...] = jnp.zeros_like(o_ref)


def _experts(xn2, blk_e, nused, slot_tok, wg, wu, wd, blk):
    T, Dh = xn2.shape
    D = 2 * Dh
    nblk = slot_tok.shape[0]
    dff = wg.shape[2]

    def wmap(b, be, nu):
        return (be[jnp.minimum(b, nu[0] - 1)], 0, 0)

    hid = pl.pallas_call(
        functools.partial(_expert_up_kernel, blk=blk),
        grid_spec=pltpu.PrefetchScalarGridSpec(
            num_scalar_prefetch=2,
            grid=(nblk,),
            in_specs=[pl.BlockSpec((1, 1, blk), lambda b, be, nu: (b, 0, 0), memory_space=pltpu.SMEM),
                      pl.BlockSpec((1, 1, blk), lambda b, be, nu: (jnp.minimum(b + 1, nu[0] - 1), 0, 0),
                                   memory_space=pltpu.SMEM),
                      pl.BlockSpec(memory_space=pl.ANY),
                      pl.BlockSpec((1, D, dff), wmap),
                      pl.BlockSpec((1, D, dff), wmap)],
            out_specs=pl.BlockSpec((blk, dff), lambda b, be, nu: (b, 0)),
            scratch_shapes=[pltpu.VMEM((2, blk, Dh), jnp.uint32), pltpu.SemaphoreType.DMA((2,))],
        ),
        out_shape=jax.ShapeDtypeStruct((nblk * blk, dff), BF16),
        compiler_params=_cparams(("arbitrary",)),
        name="experts_up",
    )(blk_e, nused, slot_tok, slot_tok, xn2, wg, wu)
    return pl.pallas_call(
        _expert_down_kernel,
        grid_spec=pltpu.PrefetchScalarGridSpec(
            num_scalar_prefetch=2,
            grid=(nblk,),
            in_specs=[pl.BlockSpec((blk, dff), lambda b, be, nu: (b, 0)),
                      pl.BlockSpec((1, dff, D), wmap)],
            out_specs=pl.BlockSpec((blk, Dh), lambda b, be, nu: (b, 0)),
        ),
        out_shape=jax.ShapeDtypeStruct((nblk * blk, Dh), jnp.uint32),
        compiler_params=_cparams(("arbitrary",)),
        name="experts_down",
    )(blk_e, nused, hid, wd)


def _combine_kernel(d_ref, dn_ref, h_ref, gw_ref, ys_hbm, o_ref, buf, sem, *, tm, nt):
    i = pl.program_id(0)
    slot = i % 2
    half = h_ref.shape[1] // 2

    @pl.when(i == 0)
    def _():
        _gather_start(d_ref, ys_hbm, buf.at[0], sem.at[0], 2 * tm)

    _gather_wait(ys_hbm, buf.at[slot], sem.at[slot], 2 * tm)
    _gather_start(dn_ref, ys_hbm, buf.at[1 - slot], sem.at[1 - slot], 2 * tm)
    lo0, hi0 = _unpack_halves(buf[slot, :tm, :])
    lo1, hi1 = _unpack_halves(buf[slot, tm:, :])
    w0 = gw_ref[:, 0:1]
    w1 = gw_ref[:, 1:2]
    o_ref[:, :half] = h_ref[:, :half] + w0 * lo0 + w1 * lo1
    o_ref[:, half:] = h_ref[:, half:] + w0 * hi0 + w1 * hi1

    @pl.when(i == nt - 1)
    def _():
        _gather_wait(ys_hbm, buf.at[1 - slot], sem.at[1 - slot], 2 * tm)


def _combine(h, ys, dest, gate, tm=128):
    T, D = h.shape
    tm = min(tm, T)
    nt = T // tm
    d3 = dest.reshape(nt, tm, 2).transpose(0, 2, 1).reshape(nt, 1, 2 * tm)
    return pl.pallas_call(
        functools.partial(_combine_kernel, tm=tm, nt=nt),
        grid=(nt,),
        in_specs=[pl.BlockSpec((1, 1, 2 * tm), lambda i: (i, 0, 0), memory_space=pltpu.SMEM),
                  pl.BlockSpec((1, 1, 2 * tm), lambda i: (jnp.minimum(i + 1, nt - 1), 0, 0),
                               memory_space=pltpu.SMEM),
                  pl.BlockSpec((tm, D), lambda i: (i, 0)),
                  pl.BlockSpec((tm, TOP_K_INNER), lambda i: (i, 0)),
                  pl.BlockSpec(memory_space=pl.ANY)],
        out_specs=pl.BlockSpec((tm, D), lambda i: (i, 0)),
        out_shape=jax.ShapeDtypeStruct((T, D), F32),
        scratch_shapes=[pltpu.VMEM((2, 2 * tm, D // 2), jnp.uint32), pltpu.SemaphoreType.DMA((2,))],
        compiler_params=_cparams(("arbitrary",)),
        name="moe_combine",
    )(d3, d3, h, gate, ys)


def _route(logits, blk):
    T = logits.shape[0]
    ng, epg, ne = N_EXPERT_GROUPS, EXPERTS_PER_GROUP, N_EXPERTS
    gp = jax.nn.softmax(logits[:, :ng], axis=-1)
    gsel = jnp.argmax(gp, axis=-1).astype(jnp.int32)
    gprob = jnp.take_along_axis(gp, gsel[:, None], axis=1)[:, 0]
    elog = logits[:, ng:ng + ne].reshape(T, ng, epg)
    elog = jnp.take_along_axis(elog, gsel[:, None, None], axis=1)[:, 0]
    topv, topi = lax.top_k(jax.nn.softmax(elog, axis=-1), TOP_K_INNER)
    gate = topv / jnp.sum(topv, axis=-1, keepdims=True) * gprob[:, None]
    eid = (gsel[:, None] * epg + topi.astype(jnp.int32)).reshape(-1)
    N = T * TOP_K_INNER
    piece = min(256, N)
    onehot = (eid[:, None] == jnp.arange(ne, dtype=jnp.int32)[None, :]).astype(F32).reshape(N // piece, piece, ne)
    hp = lax.Precision.HIGHEST
    within = jnp.einsum('ij,pjk->pik', jnp.tril(jnp.ones((piece, piece), F32)), onehot, precision=hp)
    totals = within[:, -1, :]
    before = jnp.einsum('pq,qk->pk', jnp.tril(jnp.ones((N // piece, N // piece), F32), -1), totals, precision=hp)
    csum = (within + before[:, None, :]).reshape(N, ne).astype(jnp.int32)
    rank = jnp.take_along_axis(csum, eid[:, None], axis=1)[:, 0] - 1
    counts = csum[-1]
    pcounts = ((counts + blk - 1) // blk) * blk
    pends = jnp.cumsum(pcounts)
    pstarts = pends - pcounts
    dest = pstarts[eid] + rank
    nblk = N // blk + ne
    tok = jnp.arange(N, dtype=jnp.int32) // TOP_K_INNER
    slot_tok = jnp.zeros((nblk * blk,), jnp.int32).at[dest].set(tok)
    first_slot = jnp.arange(nblk, dtype=jnp.int32)[:, None] * blk
    blk_e = jnp.clip(jnp.sum((pends[None, :] <= first_slot).astype(jnp.int32), axis=1), 0, ne - 1)
    nused = (pends[-1] // blk).astype(jnp.int32).reshape(1)
    return (blk_e, nused, slot_tok.reshape(nblk, 1, blk), dest.reshape(T, TOP_K_INNER).astype(jnp.int32),
            gate.astype(F32))


def _moe(h, norm2_g, rw, rb, wg, wu, wd, blk=256):
    xn2, logits = _norm_router(h, norm2_g, rw, rb)
    blk = min(blk, h.shape[0])
    blk_e, nused, slot_tok, dest, gate = _route(logits, blk)
    ys = _experts(xn2, blk_e, nused, slot_tok, wg, wu, wd, blk)
    return _combine(h, ys, dest, gate)


def _layer(x3, lam_init, p):
    nb, L, D = x3.shape
    T = nb * L
    x = x3.reshape(T, D)
    w_in = p['w_in']
    att_w = N_HEADS * DV
    xn = _rmsnorm_cast(x, p['norm1_g'])
    qkn = _proj_qk(xn, w_in, p['q_norm_g'], p['k_norm_g'], 2 * att_w)
    v = _matmul(xn, w_in, 2 * att_w, att_w, BF16, "in_proj_v")
    u = _matmul(xn, w_in, 3 * att_w, w_in.shape[1] - 3 * att_w, F32, "in_proj_u")
    att = _attention(qkn, v, p['lam'], p['attn_sub_g'], p['q_norm_g'], p['k_norm_g'], lam_init, nb, L)
    y = _s5(u, p['s5'](L // S5_CHUNK), nb, L)
    ssm = _glu_norm(y, p['glu_w'], p['glu_b'], p['ssm_out_g'])
    h = _out_proj(att, ssm, p['w_out'], x)
    out = _moe(h, p['norm2_g'], p['rw'], p['rb'], p['wg'], p['wu'], p['wd'])
    return out.reshape(nb, L, D)


def kernel(x_prompt, x_sample, norm1_g, w_in, q_norm_g, k_norm_g, lam_q1, lam_k1, lam_q2, lam_k2, attn_sub_g, ssm_A_re, ssm_A_im, ssm_log_dt, ssm_B_re, ssm_B_im, ssm_C_re, ssm_C_im, ssm_D, glu_w, glu_b, ssm_out_g, w_out, norm2_g, router_group_w, router_group_b, router_expert_w, router_expert_b, exp_w_gate, exp_w_up, exp_w_down):
    depth = w_in.shape[0]

    def layer_params(l):
        lam_init = 0.8 - 0.6 * math.exp(-0.3 * l)
        lam = (jnp.exp(jnp.sum(lam_q1[l].astype(F32) * lam_k1[l].astype(F32)))
               - jnp.exp(jnp.sum(lam_q2[l].astype(F32) * lam_k2[l].astype(F32))) + lam_init)
        D = w_in.shape[1]
        ng, ne = router_group_w.shape[2], router_expert_w.shape[2]
        rw = jnp.zeros((D, ROUTER_PAD), F32)
        rw = rw.at[:, :ng].set(router_group_w[l].astype(F32)).at[:, ng:ng + ne].set(router_expert_w[l].astype(F32))
        rb = jnp.zeros((1, ROUTER_PAD), F32)
        rb = rb.at[0, :ng].set(router_group_b[l].astype(F32)).at[0, ng:ng + ne].set(router_expert_b[l].astype(F32))
        s5_cache = {}

        def s5(nc):
            if nc not in s5_cache:
                s5_cache[nc] = _s5_prep(ssm_A_re[l], ssm_A_im[l], ssm_log_dt[l], ssm_B_re[l], ssm_B_im[l],
                                        ssm_C_re[l], ssm_C_im[l], ssm_D[l], nc)
            return s5_cache[nc]

        return lam_init, dict(
            norm1_g=norm1_g[l], w_in=w_in[l].astype(BF16), q_norm_g=q_norm_g[l], k_norm_g=k_norm_g[l], lam=lam,
            attn_sub_g=attn_sub_g[l], s5=s5, glu_w=glu_w[l].astype(BF16), glu_b=glu_b[l], ssm_out_g=ssm_out_g[l],
            w_out=w_out[l].astype(BF16), norm2_g=norm2_g[l], rw=rw, rb=rb,
            wg=exp_w_gate[l].astype(BF16), wu=exp_w_up[l].astype(BF16), wd=exp_w_down[l].astype(BF16))

    params = [layer_params(l) for l in range(depth)]

    def run(x):
        for lam_init, p in params:
            x = _layer(x, lam_init, p)
        return x

    return (run(x_prompt), run(x_sample))
```

```python
import functools
import math

import jax
import jax.numpy as jnp
from jax import lax
from jax.experimental import pallas as pl
from jax.experimental.pallas import tpu as pltpu

F32 = jnp.float32
BF16 = jnp.bfloat16

RMS_EPS = 1e-6
LOG2E = 1.4426950408889634
LANES = 128
SUBLANES = 8
VMEM_LIMIT = 56 * 1024 * 1024

N_HEADS = 16
DQK = 64
DV = 2 * DQK
SSM_CH = 16
SSM_STATE = 64
S5_CHUNK = 16
N_EXPERT_GROUPS = 4
EXPERTS_PER_GROUP = 8
N_EXPERTS = N_EXPERT_GROUPS * EXPERTS_PER_GROUP
TOP_K_INNER = 2
ROUTER_PAD = LANES


def _cparams(sem):
    return pltpu.CompilerParams(dimension_semantics=sem, vmem_limit_bytes=VMEM_LIMIT)


def _rmsnorm_kernel(x_ref, g_ref, o_ref):
    x = x_ref[...]
    ms = jnp.mean(x * x, axis=-1, keepdims=True)
    o_ref[...] = (x * lax.rsqrt(ms + RMS_EPS) * g_ref[...]).astype(o_ref.dtype)


def _rmsnorm_cast(x, g, tm=512):
    T, D = x.shape
    tm = min(tm, T)
    return pl.pallas_call(
        _rmsnorm_kernel,
        grid=(T // tm,),
        in_specs=[pl.BlockSpec((tm, D), lambda i: (i, 0)),
                  pl.BlockSpec((1, D), lambda i: (0, 0))],
        out_specs=pl.BlockSpec((tm, D), lambda i: (i, 0)),
        out_shape=jax.ShapeDtypeStruct((T, D), BF16),
        compiler_params=_cparams(("parallel",)),
        name="rmsnorm1",
    )(x, g.reshape(1, D).astype(F32))


def _mm_kernel(a_ref, b_ref, o_ref):
    o_ref[...] = jnp.dot(a_ref[...], b_ref[...], preferred_element_type=F32).astype(o_ref.dtype)


def _matmul(a, b, col0, ncols, out_dtype, name, tm=1024, tn=512):
    M, K = a.shape
    tm = min(tm, M)
    tn = min(tn, ncols)
    cb = col0 // tn
    return pl.pallas_call(
        _mm_kernel,
        grid=(M // tm, ncols // tn),
        in_specs=[pl.BlockSpec((tm, K), lambda i, j: (i, 0)),
                  pl.BlockSpec((K, tn), lambda i, j: (0, j + cb))],
        out_specs=pl.BlockSpec((tm, tn), lambda i, j: (i, j)),
        out_shape=jax.ShapeDtypeStruct((M, ncols), out_dtype),
        compiler_params=_cparams(("parallel", "arbitrary")),
        name=name,
    )(a, b)


def _mm_qknorm_kernel(a_ref, b_ref, g_ref, o_ref, *, tn):
    acc = jnp.dot(a_ref[...], b_ref[...], preferred_element_type=F32)
    lane = lax.broadcasted_iota(jnp.int32, (1, LANES), 1)
    left = lane < DQK
    for c in range(tn // LANES):
        x = acc[:, c * LANES:(c + 1) * LANES]
        x2 = x * x
        s_all = jnp.sum(x2, axis=-1, keepdims=True)
        s_left = jnp.sum(jnp.where(left, x2, 0.0), axis=-1, keepdims=True)
        ms = jnp.where(left, s_left, s_all - s_left) * (1.0 / DQK)
        o_ref[:, c * LANES:(c + 1) * LANES] = (
            x * lax.rsqrt(ms + RMS_EPS) * g_ref[:, c * LANES:(c + 1) * LANES]).astype(o_ref.dtype)


def _proj_qk(xn, w, gq, gk, ncols, tm=1024, tn=512):
    M, K = xn.shape
    tm = min(tm, M)
    reps = ncols // (2 * DQK)
    g = jnp.concatenate([jnp.tile(gq.astype(F32) * (DQK ** -0.5 * LOG2E), reps),
                         jnp.tile(gk.astype(F32), reps)]).reshape(1, ncols)
    return pl.pallas_call(
        functools.partial(_mm_qknorm_kernel, tn=tn),
        grid=(M // tm, ncols // tn),
        in_specs=[pl.BlockSpec((tm, K), lambda i, j: (i, 0)),
                  pl.BlockSpec((K, tn), lambda i, j: (0, j)),
                  pl.BlockSpec((1, tn), lambda i, j: (0, j))],
        out_specs=pl.BlockSpec((tm, tn), lambda i, j: (i, j)),
        out_shape=jax.ShapeDtypeStruct((M, ncols), BF16),
        compiler_params=_cparams(("parallel", "arbitrary")),
        name="in_proj_qk",
    )(xn, w, g)


POS_SPLIT = 16


def _alibi_lanes(slopes, tile):
    p = jnp.stack(_split3_bf16(slopes), axis=1)
    r = jnp.arange(tile, dtype=jnp.int32)
    pa = (r & ~(POS_SPLIT - 1)).astype(F32)
    pb = (r & (POS_SPLIT - 1)).astype(F32)
    H = slopes.shape[0]
    pos = jnp.broadcast_to(jnp.stack([pa, pa, pa, pb, pb, pb], axis=1)[None], (H, tile, 6))
    pieces = jnp.broadcast_to(jnp.concatenate([p, p], axis=1)[:, None, :], (H, tile, 6))
    pad = jnp.zeros((H, tile, LANES - 12), F32)
    qaug = jnp.concatenate([pieces, pos, pad], axis=2)
    kaug = jnp.concatenate([pos, -pieces, pad], axis=2)
    return qaug.astype(BF16), jnp.stack([kaug, -kaug], axis=1).astype(BF16), jnp.sum(p, axis=1)


def _attn_kernel(lam_ref, sl_ref, dist_ref, q_ref, k_ref, v_ref, g_ref, qaug_ref, kaug_ref, absd_ref, o_ref,
                 acc_ref, m_ref, s_ref, rm_ref, *, tq, tk, nk, out_scale):
    h = pl.program_id(1)
    qi = pl.program_id(2)
    q0 = qi * tq
    slope = sl_ref[h]
    lam = lam_ref[0]
    q = q_ref[...]
    lane = lax.broadcasted_iota(jnp.int32, (1, LANES), 1)
    zero = jnp.zeros_like(q)
    qs = jnp.concatenate([jnp.where(lane < DQK, q, zero), jnp.where(lane >= DQK, q, zero)], axis=0)
    qaug = qaug_ref[0]
    qcat = jnp.concatenate([qs, jnp.concatenate([qaug, qaug], axis=0)], axis=1)
    ones_col = jnp.broadcast_to(jnp.where(lane == 0, 1.0, 0.0).astype(BF16), (tk, LANES))
    nt = (((1,), (1,)), ((), ()))

    m_ref[...] = jnp.full(m_ref.shape, -1e30, F32)
    acc_ref[...] = jnp.zeros(acc_ref.shape, F32)

    def row_max(sp):
        return jnp.broadcast_to(jnp.max(sp, axis=-1, keepdims=True), (2 * tq, LANES))

    def update(sp, rm, c, start):
        m_old = m_ref[...]
        m_new = jnp.maximum(m_old, rm + c)
        alpha = jnp.exp2(m_old - m_new)
        mu = m_new - c
        p = jnp.exp2(sp - jnp.concatenate([mu] * (tk // LANES), axis=1)).astype(BF16)
        vcat = jnp.concatenate([v_ref[pl.ds(start, tk), :], ones_col], axis=1)
        acc_ref[...] = (jnp.concatenate([alpha, alpha], axis=1) * acc_ref[...]
                        + jnp.dot(p, vcat, preferred_element_type=F32))
        m_ref[...] = m_new

    dist = dist_ref[h]
    lo = jnp.maximum(q0 - dist, 0) // tk
    hi = jnp.minimum((q0 + tq - 1 + dist) // tk + 1, nk)
    n_left = qi - lo
    n_side = n_left + (hi - qi - 1)

    def tile_of(t):
        t = jnp.minimum(t, n_side - 1)
        return jnp.clip(jnp.where(t < n_left, lo + t, qi + 1 + t - n_left), 0, nk - 1), t < n_left

    def scores(slot, t):
        ki, is_left = tile_of(t)
        start = pl.multiple_of(ki * tk, tk)
        kaug = kaug_ref[0, jnp.where(is_left, 0, 1)]
        kc = jnp.concatenate([k_ref[pl.ds(start, tk), :], kaug], axis=1)
        sp = lax.dot_general(qcat, kc, nt, preferred_element_type=F32)
        s_ref[slot] = sp
        rm_ref[slot] = row_max(sp)

    def consume(slot, t):
        ki, is_left = tile_of(t)
        dist0 = jnp.where(is_left, q0 - ki * tk, ki * tk - q0)
        update(s_ref[slot], rm_ref[slot], -slope * lax.convert_element_type(dist0, F32),
               pl.multiple_of(ki * tk, tk))

    def pair(t0):
        scores(1, t0 + 1)
        consume(0, t0)
        scores(0, t0 + 2)
        consume(1, t0 + 1)

    def quad(i, carry):
        pair(4 * i)
        pair(4 * i + 2)
        return carry

    scores(0, 0)
    start_d = pl.multiple_of(q0, tk)
    bias = absd_ref[...] * (-slope)
    sp_d = (lax.dot_general(qs, k_ref[pl.ds(start_d, tk), :], nt, preferred_element_type=F32)
            + jnp.concatenate([bias, bias], axis=0))
    update(sp_d, row_max(sp_d), 0.0, start_d)
    n_quads = n_side // 4
    lax.fori_loop(0, n_quads, quad, 0)
    rem = n_side - 4 * n_quads

    @pl.when(rem >= 2)
    def _():
        pair(4 * n_quads)

    @pl.when(rem % 2 == 1)
    def _():
        consume(0, n_side - 1)

    acc = acc_ref[...]
    o = acc[:tq, :DV] / acc[:tq, DV:DV + 1] - lam * (acc[tq:, :DV] / acc[tq:, DV:DV + 1])
    ms = jnp.mean(o * o, axis=-1, keepdims=True)
    o_ref[...] = (o * lax.rsqrt(ms + RMS_EPS) * (g_ref[...] * out_scale)).astype(o_ref.dtype)


def _split3_bf16(x):
    p0 = x.astype(BF16).astype(F32)
    p1 = (x - p0).astype(BF16).astype(F32)
    p2 = (x - p0 - p1).astype(BF16).astype(F32)
    return p0, p1, p2


F32_EXP_ZERO = 104.0


def _zero_weight_distance(gq, gk, slopes_nat, L):
    bound = 8.0 * jnp.max(jnp.abs(gq.astype(F32))) * jnp.max(jnp.abs(gk.astype(F32))) * 1.02
    d = jnp.ceil((F32_EXP_ZERO + 2.0 * bound) / slopes_nat) + 1.0
    return jnp.clip(d, 1.0, float(L)).astype(jnp.int32)


def _attention(qkn, v, lam, sub_g, gq, gk, lam_init, nb, L, tile=512):
    T = qkn.shape[0]
    tq = tk = min(tile, L)
    nq = L // tq
    slopes_nat = jnp.exp2(-8.0 * jnp.arange(1, N_HEADS + 1, dtype=F32) / N_HEADS)
    qaug, kaug, sl = _alibi_lanes(slopes_nat * LOG2E, tq)
    idx = jnp.arange(tq, dtype=jnp.int32)
    absd = jnp.abs(idx[:, None] - idx[None, :]).astype(F32)
    dist = jnp.full((N_HEADS,), L, jnp.int32)
    kern = functools.partial(_attn_kernel, tq=tq, tk=tk, nk=L // tk, out_scale=1.0 - lam_init)
    return pl.pallas_call(
        kern,
        grid=(nb, N_HEADS, nq),
        in_specs=[pl.BlockSpec(memory_space=pltpu.SMEM),
                  pl.BlockSpec(memory_space=pltpu.SMEM),
                  pl.BlockSpec(memory_space=pltpu.SMEM),
                  pl.BlockSpec((tq, LANES), lambda b, h, i: (b * nq + i, h)),
                  pl.BlockSpec((L, LANES), lambda b, h, i: (b, N_HEADS + h)),
                  pl.BlockSpec((L, LANES), lambda b, h, i: (b, h)),
                  pl.BlockSpec((1, DV), lambda b, h, i: (0, 0)),
                  pl.BlockSpec((1, tq, LANES), lambda b, h, i: (h, 0, 0)),
                  pl.BlockSpec((1, 2, tk, LANES), lambda b, h, i: (h, 0, 0, 0)),
                  pl.BlockSpec((tq, tk), lambda b, h, i: (0, 0))],
        out_specs=pl.BlockSpec((tq, LANES), lambda b, h, i: (b * nq + i, h)),
        out_shape=jax.ShapeDtypeStruct((T, N_HEADS * DV), BF16),
        scratch_shapes=[pltpu.VMEM((2 * tq, 2 * LANES), F32), pltpu.VMEM((2 * tq, LANES), F32),
                        pltpu.VMEM((2, 2 * tq, tk), F32), pltpu.VMEM((2, 2 * tq, LANES), F32)],
        compiler_params=_cparams(("parallel", "parallel", "arbitrary")),
        name="diff_attn",
    )(lam.reshape(1).astype(F32), sl, dist, qkn, qkn, v, sub_g.reshape(1, DV).astype(F32), qaug, kaug, absd)


def _s5_prep(A_re, A_im, log_dt, B_re, B_im, C_re, C_im, d_skip, nc):
    tc = S5_CHUNK
    hp = lax.Precision.HIGHEST
    A_re, A_im, log_dt = A_re.astype(F32), A_im.astype(F32), log_dt.astype(F32)
    B_re, B_im, C_re, C_im = B_re.astype(F32), B_im.astype(F32), C_re.astype(F32), C_im.astype(F32)
    G = A_re.shape[1]
    dt = jnp.exp(log_dt)[..., None]
    ang = A_im * dt
    mag = jnp.exp(A_re * dt)
    lr, li = mag * jnp.cos(ang), mag * jnp.sin(ang)
    den = A_re * A_re + A_im * A_im
    cr = ((lr - 1.0) * A_re + li * A_im) / den
    ci = (li * A_re - (lr - 1.0) * A_im) / den
    bbr = cr[..., None] * B_re - ci[..., None] * B_im
    bbi = cr[..., None] * B_im + ci[..., None] * B_re

    def lam_pow(n):
        nn = n.astype(F32)[:, None, None, None]
        pm = jnp.exp(A_re * dt * nn)
        pa = ang * nn
        return pm * jnp.cos(pa), pm * jnp.sin(pa)

    pr, pi = lam_pow(jnp.arange(tc + 1))
    lbr = pr[..., None] * bbr - pi[..., None] * bbi
    lbi = pr[..., None] * bbi + pi[..., None] * bbr
    kk = (jnp.einsum('dghp,ndgpb->dgbnh', C_re, lbr, precision=hp)
          - jnp.einsum('dghp,ndgpb->dgbnh', C_im, lbi, precision=hp))
    centre = (kk[0, :, :, 0, :] + kk[1, :, :, 0, :]
              + jnp.eye(SSM_CH, dtype=F32)[None] * d_skip.astype(F32)[:, None, :])
    strip = jnp.concatenate([kk[1, :, :, tc - 1:0:-1, :], centre[:, :, None, :], kk[0, :, :, 1:tc, :]], axis=2)
    strip = strip.reshape(G, SSM_CH, (2 * tc - 1) * SSM_CH)
    ty = jnp.stack([strip[:, :, (tc - 1 - j) * SSM_CH:(2 * tc - 1 - j) * SSM_CH] for j in range(tc)], axis=1)
    ty = ty.reshape(G, tc * SSM_CH, tc * SSM_CH)

    def state_cols(x):
        return x.transpose(1, 0, 3, 2).reshape(G, tc * SSM_CH, SSM_STATE)

    nf = tc - 1 - jnp.arange(tc)
    nbk = jnp.arange(tc)
    w1 = jnp.concatenate([ty, state_cols(lbr[nf, 0]), state_cols(lbi[nf, 0]),
                          state_cols(lbr[nbk, 1]), state_cols(lbi[nbk, 1])], axis=-1)

    def carry_rows(n, d):
        a = C_re[d][None] * pr[n, d][:, :, None, :] - C_im[d][None] * pi[n, d][:, :, None, :]
        bco = C_re[d][None] * pi[n, d][:, :, None, :] + C_im[d][None] * pr[n, d][:, :, None, :]
        to_rows = lambda x: x.transpose(1, 3, 0, 2).reshape(G, SSM_STATE, tc * SSM_CH)
        return to_rows(a), to_rows(-bco)

    fa, fb = carry_rows(jnp.arange(tc) + 1, 0)
    ba, bb = carry_rows(tc - jnp.arange(tc), 1)
    w2 = jnp.concatenate([fa, fb, ba, bb], axis=1)

    nsteps = max(1, int(math.ceil(math.log2(nc))))
    sr, si = lam_pow(tc * (2 ** jnp.arange(nsteps)))
    rows = jnp.stack([jnp.concatenate([sr, sr], -1), jnp.concatenate([-si, si], -1)], axis=2)
    coef = rows.transpose(3, 0, 1, 2, 4).reshape(G, nsteps * 4, LANES)
    return w1.astype(BF16), w2.astype(BF16), coef, nsteps


GROUPS_PER_TILE = LANES // SSM_CH


def _granule_transpose(arrs):
    lane = lax.broadcasted_iota(jnp.int32, (1, LANES), 1)
    a = list(arrs)
    for dist, shift in ((4, 4 * SSM_CH), (2, 2 * SSM_CH), (1, SSM_CH)):
        keep_lo = (lane & (2 * shift - 1)) < shift
        new = list(a)
        for i in range(GROUPS_PER_TILE):
            if (i // dist) % 2 == 0:
                lo, hi = a[i], a[i + dist]
                new[i] = jnp.where(keep_lo, lo, pltpu.roll(hi, shift, axis=1))
                new[i + dist] = jnp.where(keep_lo, pltpu.roll(lo, LANES - shift, axis=1), hi)
        a = new
    return a


def _s5_kernel(u_ref, w1_ref, w2_ref, c_ref, o_ref, ug_ref, yg_ref, *, nc, nsteps):
    ny = S5_CHUNK * SSM_CH
    halves = S5_CHUNK // GROUPS_PER_TILE
    for hf in range(halves):
        xs = [u_ref[pl.ds(GROUPS_PER_TILE * hf + j, nc, stride=S5_CHUNK), :] for j in range(GROUPS_PER_TILE)]
        for g, x in enumerate(_granule_transpose(xs)):
            ug_ref[g, :, hf * LANES:(hf + 1) * LANES] = x.astype(BF16)

    row = lax.broadcasted_iota(jnp.int32, (nc, 1), 0)

    def shifted(x, d, fwd):
        if d % SUBLANES == 0:
            pad = jnp.zeros((d, x.shape[1]), x.dtype)
            return jnp.concatenate([pad, x[:nc - d]] if fwd else [x[d:], pad], axis=0)
        if fwd:
            return jnp.where(row >= d, pltpu.roll(x, d, axis=0), 0.0)
        return jnp.where(row < nc - d, pltpu.roll(x, nc - d, axis=0), 0.0)

    def group(g, carry):
        r = jnp.dot(ug_ref[g], w1_ref[g], preferred_element_type=F32)

        def carry_in(x, dirn):
            s = shifted(x, 1, dirn == 0)
            for st in range(nsteps):
                d = 2 ** st
                if d >= nc:
                    break
                sh = shifted(s, d, dirn == 0)
                a = c_ref[g, 4 * st + 2 * dirn:4 * st + 2 * dirn + 1, :]
                b = c_ref[g, 4 * st + 2 * dirn + 1:4 * st + 2 * dirn + 2, :]
                s = s + a * sh + b * pltpu.roll(sh, SSM_STATE, axis=1)
            return s

        sf = carry_in(r[:, ny:ny + LANES], 0)
        sb = carry_in(r[:, ny + LANES:ny + 2 * LANES], 1)
        s = jnp.concatenate([sf, sb], axis=1).astype(BF16)
        yg_ref[g] = r[:, :ny] + jnp.dot(s, w2_ref[g], preferred_element_type=F32)
        return carry

    lax.fori_loop(0, GROUPS_PER_TILE, group, 0)

    for hf in range(halves):
        ys = [yg_ref[g, :, hf * LANES:(hf + 1) * LANES] for g in range(GROUPS_PER_TILE)]
        for j, x in enumerate(_granule_transpose(ys)):
            o_ref[pl.ds(GROUPS_PER_TILE * hf + j, nc, stride=S5_CHUNK), :] = x


def _s5(u, prep, nb, L):
    w1, w2, coef, nsteps = prep
    T, W = u.shape
    gt = GROUPS_PER_TILE
    nc = L // S5_CHUNK
    ny = S5_CHUNK * SSM_CH
    return pl.pallas_call(
        functools.partial(_s5_kernel, nc=nc, nsteps=nsteps),
        grid=(nb, W // LANES),
        in_specs=[pl.BlockSpec((L, LANES), lambda b, t: (b, t), pipeline_mode=pl.Buffered(1)),
                  pl.BlockSpec((gt, ny, 2 * ny), lambda b, t: (t, 0, 0)),
                  pl.BlockSpec((gt, ny, ny), lambda b, t: (t, 0, 0)),
                  pl.BlockSpec((gt, 4 * nsteps, LANES), lambda b, t: (t, 0, 0))],
        out_specs=pl.BlockSpec((L, LANES), lambda b, t: (b, t), pipeline_mode=pl.Buffered(1)),
        out_shape=jax.ShapeDtypeStruct((T, W), F32),
        scratch_shapes=[pltpu.VMEM((gt, nc, ny), BF16), pltpu.VMEM((gt, nc, ny), F32)],
        compiler_params=_cparams(("parallel", "parallel")),
        name="s5_chunked",
    )(u, w1, w2, coef)


def _glu_kernel(y_ref, w_ref, b_ref, g_ref, o_ref):
    y = y_ref[...]
    c0 = math.sqrt(2.0 / math.pi)
    gl = 0.5 * y * (1.0 + jnp.tanh(c0 * (y + 0.044715 * (y * y * y))))
    z = jnp.dot(gl.astype(BF16), w_ref[...], preferred_element_type=F32) + b_ref[...]
    s = gl * (1.0 / (1.0 + jnp.exp(-z)))
    ms = jnp.mean(s * s, axis=-1, keepdims=True)
    o_ref[...] = (s * lax.rsqrt(ms + RMS_EPS) * g_ref[...]).astype(o_ref.dtype)


def _glu_norm(y, w, b, g, tm=512):
    T, W = y.shape
    tm = min(tm, T)
    return pl.pallas_call(
        _glu_kernel,
        grid=(T // tm,),
        in_specs=[pl.BlockSpec((tm, W), lambda i: (i, 0)),
                  pl.BlockSpec((W, W), lambda i: (0, 0)),
                  pl.BlockSpec((1, W), lambda i: (0, 0)),
                  pl.BlockSpec((1, W), lambda i: (0, 0))],
        out_specs=pl.BlockSpec((tm, W), lambda i: (i, 0)),
        out_shape=jax.ShapeDtypeStruct((T, W), BF16),
        compiler_params=_cparams(("parallel",)),
        name="glu_norm",
    )(y, w, b.reshape(1, W).astype(F32), g.reshape(1, W).astype(F32))


def _out_proj_kernel(a_ref, s_ref, wa_ref, ws_ref, x_ref, o_ref):
    acc = jnp.dot(a_ref[...], wa_ref[...], preferred_element_type=F32)
    acc = acc + jnp.dot(s_ref[...], ws_ref[...], preferred_element_type=F32)
    o_ref[...] = x_ref[...] + acc


def _out_proj(att, ssm, w, x, tm=1024, tn=512):
    T, Wa = att.shape
    D = x.shape[1]
    tm = min(tm, T)
    tn = min(tn, D)
    rb = 1
    return pl.pallas_call(
        _out_proj_kernel,
        grid=(T // tm, D // tn),
        in_specs=[pl.BlockSpec((tm, Wa), lambda i, j: (i, 0)),
                  pl.BlockSpec((tm, Wa), lambda i, j: (i, 0)),
                  pl.BlockSpec((Wa, tn), lambda i, j: (0, j)),
                  pl.BlockSpec((Wa, tn), lambda i, j: (rb, j)),
                  pl.BlockSpec((tm, tn), lambda i, j: (i, j))],
        out_specs=pl.BlockSpec((tm, tn), lambda i, j: (i, j)),
        out_shape=jax.ShapeDtypeStruct((T, D), F32),
        compiler_params=_cparams(("parallel", "arbitrary")),
        name="out_proj",
    )(att, ssm, w, w, x)


def _router_kernel(h_ref, g_ref, w_ref, b_ref, xn_ref, lg_ref):
    x = h_ref[...]
    ms = jnp.mean(x * x, axis=-1, keepdims=True)
    xn = x * lax.rsqrt(ms + RMS_EPS) * g_ref[...]
    xn_ref[...] = _pack_halves(xn)
    lg_ref[...] = jnp.dot(xn, w_ref[...], preferred_element_type=F32,
                          precision=lax.Precision.HIGHEST) + b_ref[...]


def _norm_router(h, g, rw, rb, tm=256):
    T, D = h.shape
    tm = min(tm, T)
    return pl.pallas_call(
        _router_kernel,
        grid=(T // tm,),
        in_specs=[pl.BlockSpec((tm, D), lambda i: (i, 0)),
                  pl.BlockSpec((1, D), lambda i: (0, 0)),
                  pl.BlockSpec((D, ROUTER_PAD), lambda i: (0, 0)),
                  pl.BlockSpec((1, ROUTER_PAD), lambda i: (0, 0))],
        out_specs=[pl.BlockSpec((tm, D // 2), lambda i: (i, 0)),
                   pl.BlockSpec((tm, ROUTER_PAD), lambda i: (i, 0))],
        out_shape=[jax.ShapeDtypeStruct((T, D // 2), jnp.uint32), jax.ShapeDtypeStruct((T, ROUTER_PAD), F32)],
        compiler_params=_cparams(("parallel",)),
        name="norm_router",
    )(h, g.reshape(1, D).astype(F32), rw, rb)


def _pack_halves(x):
    half = x.shape[1] // 2
    lo = pltpu.bitcast(x[:, :half].astype(BF16).astype(F32), jnp.uint32)
    hi = pltpu.bitcast(x[:, half:].astype(BF16).astype(F32), jnp.uint32)
    return hi | (lo >> 16)


def _unpack_halves(w):
    lo = pltpu.bitcast(w << 16, F32)
    hi = pltpu.bitcast(w & jnp.uint32(0xFFFF0000), F32)
    return lo, hi


def _row_copy(src_hbm, src_row, dst, dst_row, sem):
    return pltpu.make_async_copy(src_hbm.at[pl.ds(src_row, 1), :], dst.at[pl.ds(dst_row, 1), :], sem)


def _gather_start(idx_ref, src_hbm, dst, sem, n):
    for r in range(n):
        _row_copy(src_hbm, idx_ref[0, 0, r], dst, r, sem).start()


def _gather_wait(src_hbm, dst, sem, n):
    for r in range(n):
        _row_copy(src_hbm, 0, dst, r, sem).wait()


def _expert_up_kernel(be_ref, nu_ref, tok_ref, tokn_ref, x_hbm, wg_ref, wu_ref, h_ref, xbuf, sem, *, blk):
    b = pl.program_id(0)
    nused = nu_ref[0]
    slot = b % 2

    @pl.when(b == 0)
    def _():
        _gather_start(tok_ref, x_hbm, xbuf.at[0], sem.at[0], blk)

    @pl.when(b < nused)
    def _():
        _gather_wait(x_hbm, xbuf.at[slot], sem.at[slot], blk)
        _gather_start(tokn_ref, x_hbm, xbuf.at[1 - slot], sem.at[1 - slot], blk)
        lo, hi = _unpack_halves(xbuf[slot])
        x = jnp.concatenate([lo, hi], axis=1).astype(BF16)
        g = jnp.dot(x, wg_ref[0], preferred_element_type=F32)
        u = jnp.dot(x, wu_ref[0], preferred_element_type=F32)
        h_ref[...] = (g * (1.0 / (1.0 + jnp.exp(-g))) * u).astype(h_ref.dtype)

    @pl.when(b >= nused)
    def _():
        h_ref[...] = jnp.zeros_like(h_ref)

    @pl.when(b == nused)
    def _():
        _gather_wait(x_hbm, xbuf.at[slot], sem.at[slot], blk)


def _expert_down_kernel(be_ref, nu_ref, h_ref, wd_ref, o_ref):
    live = pl.program_id(0) < nu_ref[0]

    @pl.when(live)
    def _():
        o_ref[...] = _pack_halves(jnp.dot(h_ref[...], wd_ref[0], preferred_element_type=F32))

    @pl.when(jnp.logical_not(live))
    def _():
        o_ref[...] = jnp.zeros_like(o_ref)


def _experts(xn2, blk_e, nused, slot_tok, wg, wu, wd, blk):
    T, Dh = xn2.shape
    D = 2 * Dh
    nblk = slot_tok.shape[0]
    dff = wg.shape[2]

    def wmap(b, be, nu):
        return (be[jnp.minimum(b, nu[0] - 1)], 0, 0)

    hid = pl.pallas_call(
        functools.partial(_expert_up_kernel, blk=blk),
        grid_spec=pltpu.PrefetchScalarGridSpec(
            num_scalar_prefetch=2,
            grid=(nblk,),
            in_specs=[pl.BlockSpec((1, 1, blk), lambda b, be, nu: (b, 0, 0), memory_space=pltpu.SMEM),
                      pl.BlockSpec((1, 1, blk), lambda b, be, nu: (jnp.minimum(b + 1, nu[0] - 1), 0, 0),
                                   memory_space=pltpu.SMEM),
                      pl.BlockSpec(memory_space=pl.ANY),
                      pl.BlockSpec((1, D, dff), wmap),
                      pl.BlockSpec((1, D, dff), wmap)],
            out_specs=pl.BlockSpec((blk, dff), lambda b, be, nu: (b, 0)),
            scratch_shapes=[pltpu.VMEM((2, blk, Dh), jnp.uint32), pltpu.SemaphoreType.DMA((2,))],
        ),
        out_shape=jax.ShapeDtypeStruct((nblk * blk, dff), BF16),
        compiler_params=_cparams(("arbitrary",)),
        name="experts_up",
    )(blk_e, nused, slot_tok, slot_tok, xn2, wg, wu)
    return pl.pallas_call(
        _expert_down_kernel,
        grid_spec=pltpu.PrefetchScalarGridSpec(
            num_scalar_prefetch=2,
            grid=(nblk,),
            in_specs=[pl.BlockSpec((blk, dff), lambda b, be, nu: (b, 0)),
                      pl.BlockSpec((1, dff, D), wmap)],
            out_specs=pl.BlockSpec((blk, Dh), lambda b, be, nu: (b, 0)),
        ),
        out_shape=jax.ShapeDtypeStruct((nblk * blk, Dh), jnp.uint32),
        compiler_params=_cparams(("arbitrary",)),
        name="experts_down",
    )(blk_e, nused, hid, wd)


def _combine_kernel(d_ref, dn_ref, h_ref, gw_ref, ys_hbm, o_ref, buf, sem, *, tm, nt):
    i = pl.program_id(0)
    slot = i % 2
    half = h_ref.shape[1] // 2

    @pl.when(i == 0)
    def _():
        _gather_start(d_ref, ys_hbm, buf.at[0], sem.at[0], 2 * tm)

    _gather_wait(ys_hbm, buf.at[slot], sem.at[slot], 2 * tm)
    _gather_start(dn_ref, ys_hbm, buf.at[1 - slot], sem.at[1 - slot], 2 * tm)
    lo0, hi0 = _unpack_halves(buf[slot, :tm, :])
    lo1, hi1 = _unpack_halves(buf[slot, tm:, :])
    w0 = gw_ref[:, 0:1]
    w1 = gw_ref[:, 1:2]
    o_ref[:, :half] = h_ref[:, :half] + w0 * lo0 + w1 * lo1
    o_ref[:, half:] = h_ref[:, half:] + w0 * hi0 + w1 * hi1

    @pl.when(i == nt - 1)
    def _():
        _gather_wait(ys_hbm, buf.at[1 - slot], sem.at[1 - slot], 2 * tm)


def _combine(h, ys, dest, gate, tm=128):
    T, D = h.shape
    tm = min(tm, T)
    nt = T // tm
    d3 = dest.reshape(nt, tm, 2).transpose(0, 2, 1).reshape(nt, 1, 2 * tm)
    return pl.pallas_call(
        functools.partial(_combine_kernel, tm=tm, nt=nt),
        grid=(nt,),
        in_specs=[pl.BlockSpec((1, 1, 2 * tm), lambda i: (i, 0, 0), memory_space=pltpu.SMEM),
                  pl.BlockSpec((1, 1, 2 * tm), lambda i: (jnp.minimum(i + 1, nt - 1), 0, 0),
                               memory_space=pltpu.SMEM),
                  pl.BlockSpec((tm, D), lambda i: (i, 0)),
                  pl.BlockSpec((tm, TOP_K_INNER), lambda i: (i, 0)),
                  pl.BlockSpec(memory_space=pl.ANY)],
        out_specs=pl.BlockSpec((tm, D), lambda i: (i, 0)),
        out_shape=jax.ShapeDtypeStruct((T, D), F32),
        scratch_shapes=[pltpu.VMEM((2, 2 * tm, D // 2), jnp.uint32), pltpu.SemaphoreType.DMA((2,))],
        compiler_params=_cparams(("arbitrary",)),
        name="moe_combine",
    )(d3, d3, h, gate, ys)


def _route(logits, blk):
    T = logits.shape[0]
    ng, epg, ne = N_EXPERT_GROUPS, EXPERTS_PER_GROUP, N_EXPERTS
    gp = jax.nn.softmax(logits[:, :ng], axis=-1)
    gsel = jnp.argmax(gp, axis=-1).astype(jnp.int32)
    gprob = jnp.take_along_axis(gp, gsel[:, None], axis=1)[:, 0]
    elog = logits[:, ng:ng + ne].reshape(T, ng, epg)
    elog = jnp.take_along_axis(elog, gsel[:, None, None], axis=1)[:, 0]
    topv, topi = lax.top_k(jax.nn.softmax(elog, axis=-1), TOP_K_INNER)
    gate = topv / jnp.sum(topv, axis=-1, keepdims=True) * gprob[:, None]
    eid = (gsel[:, None] * epg + topi.astype(jnp.int32)).reshape(-1)
    N = T * TOP_K_INNER
    piece = min(256, N)
    onehot = (eid[:, None] == jnp.arange(ne, dtype=jnp.int32)[None, :]).astype(F32).reshape(N // piece, piece, ne)
    hp = lax.Precision.HIGHEST
    within = jnp.einsum('ij,pjk->pik', jnp.tril(jnp.ones((piece, piece), F32)), onehot, precision=hp)
    totals = within[:, -1, :]
    before = jnp.einsum('pq,qk->pk', jnp.tril(jnp.ones((N // piece, N // piece), F32), -1), totals, precision=hp)
    csum = (within + before[:, None, :]).reshape(N, ne).astype(jnp.int32)
    rank = jnp.take_along_axis(csum, eid[:, None], axis=1)[:, 0] - 1
    counts = csum[-1]
    pcounts = ((counts + blk - 1) // blk) * blk
    pends = jnp.cumsum(pcounts)
    pstarts = pends - pcounts
    dest = pstarts[eid] + rank
    nblk = N // blk + ne
    tok = jnp.arange(N, dtype=jnp.int32) // TOP_K_INNER
    slot_tok = jnp.zeros((nblk * blk,), jnp.int32).at[dest].set(tok)
    first_slot = jnp.arange(nblk, dtype=jnp.int32)[:, None] * blk
    blk_e = jnp.clip(jnp.sum((pends[None, :] <= first_slot).astype(jnp.int32), axis=1), 0, ne - 1)
    nused = (pends[-1] // blk).astype(jnp.int32).reshape(1)
    return (blk_e, nused, slot_tok.reshape(nblk, 1, blk), dest.reshape(T, TOP_K_INNER).astype(jnp.int32),
            gate.astype(F32))


def _moe(h, norm2_g, rw, rb, wg, wu, wd, blk=256):
    xn2, logits = _norm_router(h, norm2_g, rw, rb)
    blk = min(blk, h.shape[0])
    blk_e, nused, slot_tok, dest, gate = _route(logits, blk)
    ys = _experts(xn2, blk_e, nused, slot_tok, wg, wu, wd, blk)
    return _combine(h, ys, dest, gate)


def _layer(x3, lam_init, p):
    nb, L, D = x3.shape
    T = nb * L
    x = x3.reshape(T, D)
    w_in = p['w_in']
    att_w = N_HEADS * DV
    xn = _rmsnorm_cast(x, p['norm1_g'])
    qkn = _proj_qk(xn, w_in, p['q_norm_g'], p['k_norm_g'], 2 * att_w)
    v = _matmul(xn, w_in, 2 * att_w, att_w, BF16, "in_proj_v")
    u = _matmul(xn, w_in, 3 * att_w, w_in.shape[1] - 3 * att_w, F32, "in_proj_u")
    att = _attention(qkn, v, p['lam'], p['attn_sub_g'], p['q_norm_g'], p['k_norm_g'], lam_init, nb, L)
    y = _s5(u, p['s5'](L // S5_CHUNK), nb, L)
    ssm = _glu_norm(y, p['glu_w'], p['glu_b'], p['ssm_out_g'])
    h = _out_proj(att, ssm, p['w_out'], x)
    out = _moe(h, p['norm2_g'], p['rw'], p['rb'], p['wg'], p['wu'], p['wd'])
    return out.reshape(nb, L, D)


def kernel(x_prompt, x_sample, norm1_g, w_in, q_norm_g, k_norm_g, lam_q1, lam_k1, lam_q2, lam_k2, attn_sub_g, ssm_A_re, ssm_A_im, ssm_log_dt, ssm_B_re, ssm_B_im, ssm_C_re, ssm_C_im, ssm_D, glu_w, glu_b, ssm_out_g, w_out, norm2_g, router_group_w, router_group_b, router_expert_w, router_expert_b, exp_w_gate, exp_w_up, exp_w_down):
    depth = w_in.shape[0]

    def layer_params(l):
        lam_init = 0.8 - 0.6 * math.exp(-0.3 * l)
        lam = (jnp.exp(jnp.sum(lam_q1[l].astype(F32) * lam_k1[l].astype(F32)))
               - jnp.exp(jnp.sum(lam_q2[l].astype(F32) * lam_k2[l].astype(F32))) + lam_init)
        D = w_in.shape[1]
        ng, ne = router_group_w.shape[2], router_expert_w.shape[2]
        rw = jnp.zeros((D, ROUTER_PAD), F32)
        rw = rw.at[:, :ng].set(router_group_w[l].astype(F32)).at[:, ng:ng + ne].set(router_expert_w[l].astype(F32))
        rb = jnp.zeros((1, ROUTER_PAD), F32)
        rb = rb.at[0, :ng].set(router_group_b[l].astype(F32)).at[0, ng:ng + ne].set(router_expert_b[l].astype(F32))
        s5_cache = {}

        def s5(nc):
            if nc not in s5_cache:
                s5_cache[nc] = _s5_prep(ssm_A_re[l], ssm_A_im[l], ssm_log_dt[l], ssm_B_re[l], ssm_B_im[l],
                                        ssm_C_re[l], ssm_C_im[l], ssm_D[l], nc)
            return s5_cache[nc]

        return lam_init, dict(
            norm1_g=norm1_g[l], w_in=w_in[l].astype(BF16), q_norm_g=q_norm_g[l], k_norm_g=k_norm_g[l], lam=lam,
            attn_sub_g=attn_sub_g[l], s5=s5, glu_w=glu_w[l].astype(BF16), glu_b=glu_b[l], ssm_out_g=ssm_out_g[l],
            w_out=w_out[l].astype(BF16), norm2_g=norm2_g[l], rw=rw, rb=rb,
            wg=exp_w_gate[l].astype(BF16), wu=exp_w_up[l].astype(BF16), wd=exp_w_down[l].astype(BF16))

    params = [layer_params(l) for l in range(depth)]

    def run(x):
        for lam_init, p in params:
            x = _layer(x, lam_init, p)
        return x

    return (run(x_prompt), run(x_sample))
```
